```python
import jax, jax.numpy as jnp
from jax import lax
import numpy as np

D_MODEL = 2048
BATCH = 16
SEQ = 2048
DEPTH = 4
DEC_BATCH = 32
DEC_SEQ = 64
PAST_LEN = 1024

CHUNK = 64
N_POOL_LAYERS = DEPTH // 2
N_ATTN_LAYERS = DEPTH - N_POOL_LAYERS
POOL_WINDOWS = (2, 4, 8, 16)
POOL_GROUPS = len(POOL_WINDOWS)
POOL_GROUP_DIM = D_MODEL // POOL_GROUPS
POOL_HIST = max(POOL_WINDOWS) - 1
N_HEADS = 16
HEAD_DIM = D_MODEL // N_HEADS
LEFT_CHUNKS = 8
LEFT_LEN = LEFT_CHUNKS * CHUNK
BAND_LEN = LEFT_LEN + CHUNK
REL_CLIP = 128
N_GROUPS = 4
EXPERTS_PER_GROUP = 8
N_EXPERTS = N_GROUPS * EXPERTS_PER_GROUP
TOP_K = 2
D_EXPERT = D_MODEL // 4
EXPERT_BLOCK = 256
RMS_EPS = 1e-6
NEG_INF = -1e30

kernel_name = "yoco_pool_chunkband_hmoe_stream_step"


def rmsnorm(x, g):
    xf = x.astype(jnp.float32)
    y = xf * lax.rsqrt(jnp.mean(xf * xf, axis=-1, keepdims=True) + RMS_EPS)
    return y.astype(x.dtype) * g


def pool_mixer(h, hist, pos0, w_pool, s_pool):
    B, L, D = h.shape
    full = jnp.concatenate([hist, h], axis=1).astype(jnp.float32)
    csum = jnp.concatenate([jnp.zeros((B, 1, D), jnp.float32), jnp.cumsum(full, axis=1)], axis=1)
    end = csum[:, POOL_HIST + 1:]
    pos = pos0 + jnp.arange(L, dtype=jnp.int32)
    outs = []
    for gi, w in enumerate(POOL_WINDOWS):
        lo, hi = gi * POOL_GROUP_DIM, (gi + 1) * POOL_GROUP_DIM
        start = csum[:, POOL_HIST + 1 - w:POOL_HIST + 1 - w + L, lo:hi]
        cnt = jnp.minimum(pos + 1, w).astype(jnp.float32)[None, :, None]
        outs.append((end[..., lo:hi] - start) / cnt)
    pooled = (jnp.concatenate(outs, axis=-1) - h.astype(jnp.float32)).astype(h.dtype)
    pooled = pooled.reshape(B, L, POOL_GROUPS, POOL_GROUP_DIM)
    mixed = jnp.einsum('blgc,gcd->blgd', pooled, w_pool).reshape(B, L, D) * s_pool
    new_hist = full[:, -POOL_HIST:].astype(h.dtype)
    return mixed, new_hist


def attend(q, k, v, q_pos, k_pos, table):
    rel = jnp.clip(q_pos[:, None] - k_pos[None, :], -REL_CLIP, REL_CLIP) + REL_CLIP
    bias = table[:, rel].astype(jnp.float32)
    s = jnp.einsum('bqhd,bkhd->bhqk', q, k, preferred_element_type=jnp.float32) * (HEAD_DIM ** -0.5)
    s = jnp.where((k_pos >= 0)[None, None, None, :], s + bias[None], NEG_INF)
    p = jax.nn.softmax(s, axis=-1).astype(v.dtype)
    return jnp.einsum('bhqk,bkhd->bqhd', p, v)


def band_attention(q, k, v, table):
    B, S, H, Dh = q.shape
    n_chunks = S // CHUNK
    pad = ((0, 0), (LEFT_LEN, 0), (0, 0), (0, 0))
    kp = jnp.pad(k, pad)
    vp = jnp.pad(v, pad)

    def one_chunk(c):
        q0 = c * CHUNK
        qc = lax.dynamic_slice_in_dim(q, q0, CHUNK, axis=1)
        kc = lax.dynamic_slice_in_dim(kp, q0, BAND_LEN, axis=1)
        vc = lax.dynamic_slice_in_dim(vp, q0, BAND_LEN, axis=1)
        q_pos = q0 + jnp.arange(CHUNK, dtype=jnp.int32)
        k_pos = q0 - LEFT_LEN + jnp.arange(BAND_LEN, dtype=jnp.int32)
        return attend(qc, kc, vc, q_pos, k_pos, table)

    out = lax.map(one_chunk, jnp.arange(n_chunks, dtype=jnp.int32))
    return out.transpose(1, 0, 2, 3, 4).reshape(B, S, H, Dh)


def stream_attention(q, k_new, v_new, k_past, v_past, pos0, table):
    L = q.shape[1]
    P = k_past.shape[1]
    k_all = jnp.concatenate([k_past, k_new], axis=1)
    v_all = jnp.concatenate([v_past, v_new], axis=1)
    q_pos = pos0 + jnp.arange(L, dtype=jnp.int32)
    k_pos = pos0 - P + jnp.arange(P + L, dtype=jnp.int32)
    return attend(q, k_all, v_all, q_pos, k_pos, table)


def grouped_expert_mlp(t, expert, gate, w_gate, w_up, w_down):
    n_tok, d = t.shape
    n_assign = n_tok * TOP_K
    e_flat = expert.reshape(n_assign)
    g_flat = gate.reshape(n_assign).astype(jnp.float32)
    tok = jnp.arange(n_assign, dtype=jnp.int32) // TOP_K
    counts = jnp.zeros((N_EXPERTS,), jnp.int32).at[e_flat].add(1)
    padded = (counts + EXPERT_BLOCK - 1) // EXPERT_BLOCK * EXPERT_BLOCK
    starts = jnp.cumsum(counts) - counts
    padded_ends = jnp.cumsum(padded)
    padded_starts = padded_ends - padded
    order = jnp.argsort(e_flat, stable=True)
    e_sorted = e_flat[order]
    rank = jnp.arange(n_assign, dtype=jnp.int32) - starts[e_sorted]
    dest = padded_starts[e_sorted] + rank
    n_blocks = -(-(n_assign + N_EXPERTS * (EXPERT_BLOCK - 1)) // EXPERT_BLOCK)
    n_slots = n_blocks * EXPERT_BLOCK
    slot_tok = jnp.full((n_slots,), n_tok, jnp.int32).at[dest].set(tok[order])
    slot_gate = jnp.zeros((n_slots,), jnp.float32).at[dest].set(g_flat[order])
    block_start = jnp.arange(n_blocks, dtype=jnp.int32) * EXPERT_BLOCK
    block_expert = jnp.minimum(jnp.searchsorted(padded_ends, block_start, side='right'), N_EXPERTS - 1)
    t_pad = jnp.concatenate([t, jnp.zeros((1, d), t.dtype)], axis=0)
    xs = t_pad[slot_tok].reshape(n_blocks, EXPERT_BLOCK, d)

    def block_mlp(args):
        xb, e = args
        return (jax.nn.silu(xb @ w_gate[e]) * (xb @ w_up[e])) @ w_down[e]

    ys = lax.map(block_mlp, (xs, block_expert)).reshape(n_slots, d)
    out = jnp.zeros((n_tok + 1, d), jnp.float32).at[slot_tok].add(ys.astype(jnp.float32) * slot_gate[:, None])
    return out[:n_tok].astype(t.dtype)


def hier_moe(h, w_group, b_group, w_router, b_router, w_gate, w_up, w_down):
    B, L, D = h.shape
    n_tok = B * L
    t = h.reshape(n_tok, D)
    g_logits = (t @ w_group + b_group).astype(jnp.float32)
    g_sel = jnp.argmax(g_logits, axis=-1).astype(jnp.int32)
    g_prob = jnp.max(jax.nn.softmax(g_logits, axis=-1), axis=-1, keepdims=True)
    e_logits = (t @ w_router + b_router).astype(jnp.float32).reshape(n_tok, N_GROUPS, EXPERTS_PER_GROUP)
    e_logits = e_logits[jnp.arange(n_tok), g_sel]
    top_v, top_i = lax.top_k(e_logits, TOP_K)
    gate = jax.nn.softmax(top_v, axis=-1) * g_prob
    expert = g_sel[:, None] * EXPERTS_PER_GROUP + top_i.astype(jnp.int32)
    return grouped_expert_mlp(t, expert, gate, w_gate, w_up, w_down).reshape(B, L, D)


def trunk(x, pool_hist, k_past, v_past, pos0, g_pool, w_pool, s_pool, g_kv, w_kv,
          g_attn, w_q, w_o, rel_bias, g_ffn, w_group, b_group, w_router, b_router,
          w_gate, w_up, w_down, g_final):
    B, L, _ = x.shape
    new_hist = []
    k = v = None
    for layer in range(DEPTH):
        if layer < N_POOL_LAYERS:
            h = rmsnorm(x, g_pool[layer])
            hist = jnp.zeros((B, POOL_HIST, D_MODEL), h.dtype) if pool_hist is None else pool_hist[layer]
            mix, nh = pool_mixer(h, hist, pos0, w_pool[layer], s_pool[layer])
            new_hist.append(nh)
        else:
            j = layer - N_POOL_LAYERS
            if j == 0:
                kv = rmsnorm(x, g_kv) @ w_kv
                k = kv[..., :D_MODEL].reshape(B, L, N_HEADS, HEAD_DIM)
                v = kv[..., D_MODEL:].reshape(B, L, N_HEADS, HEAD_DIM)
            h = rmsnorm(x, g_attn[j])
            q = (h @ w_q[j]).reshape(B, L, N_HEADS, HEAD_DIM)
            if k_past is None:
                o = band_attention(q, k, v, rel_bias[j])
            else:
                o = stream_attention(q, k, v, k_past, v_past, pos0, rel_bias[j])
            mix = o.reshape(B, L, D_MODEL) @ w_o[j]
        x = x + mix
        x = x + hier_moe(rmsnorm(x, g_ffn[layer]), w_group[layer], b_group[layer], w_router[layer],
                         b_router[layer], w_gate[layer], w_up[layer], w_down[layer])
    return rmsnorm(x, g_final), jnp.stack(new_hist, axis=0), k, v


def setup_inputs(seed: int = 0) -> dict:
    key = jax.random.key(seed)
    ks = jax.random.split(key, 24)
    f32 = jnp.float32

    def nrm(k, shape, scale):
        return jax.random.normal(k, shape, f32) * scale

    cache_len = min(LEFT_LEN, PAST_LEN)
    D = D_MODEL
    return {
        "x_prompt": nrm(ks[0], (BATCH, SEQ, D), 1.0),
        "x_sample": nrm(ks[1], (DEC_BATCH, DEC_SEQ, D), 1.0),
        "state_pool": nrm(ks[2], (N_POOL_LAYERS, DEC_BATCH, POOL_HIST, D), 1.0),
        "cache_k": nrm(ks[3], (DEC_BATCH, cache_len, N_HEADS, HEAD_DIM), 1.0),
        "cache_v": nrm(ks[4], (DEC_BATCH, cache_len, N_HEADS, HEAD_DIM), 1.0),
        "g_pool": 1.0 + nrm(ks[5], (N_POOL_LAYERS, D), 0.02),
        "w_pool": nrm(ks[6], (N_POOL_LAYERS, POOL_GROUPS, POOL_GROUP_DIM, POOL_GROUP_DIM), POOL_GROUP_DIM ** -0.5),
        "s_pool": 1.0 + nrm(ks[7], (N_POOL_LAYERS, D), 0.02),
        "g_kv": 1.0 + nrm(ks[8], (D,), 0.02),
        "w_kv": nrm(ks[9], (D, 2 * D), D ** -0.5),
        "g_attn": 1.0 + nrm(ks[10], (N_ATTN_LAYERS, D), 0.02),
        "w_q": nrm(ks[11], (N_ATTN_LAYERS, D, D), D ** -0.5),
        "w_o": nrm(ks[12], (N_ATTN_LAYERS, D, D), D ** -0.5),
        "rel_bias": nrm(ks[13], (N_ATTN_LAYERS, N_HEADS, 2 * REL_CLIP + 1), 0.2),
        "g_ffn": 1.0 + nrm(ks[14], (DEPTH, D), 0.02),
        "w_group": nrm(ks[15], (DEPTH, D, N_GROUPS), D ** -0.5),
        "b_group": nrm(ks[16], (DEPTH, N_GROUPS), 0.01),
        "w_router": nrm(ks[17], (DEPTH, D, N_EXPERTS), D ** -0.5),
        "b_router": nrm(ks[18], (DEPTH, N_EXPERTS), 0.01),
        "w_gate": nrm(ks[19], (DEPTH, N_EXPERTS, D, D_EXPERT), D ** -0.5),
        "w_up": nrm(ks[20], (DEPTH, N_EXPERTS, D, D_EXPERT), D ** -0.5),
        "w_down": nrm(ks[21], (DEPTH, N_EXPERTS, D_EXPERT, D), D_EXPERT ** -0.5),
        "g_final": 1.0 + nrm(ks[22], (D,), 0.02),
    }


def reference(x_prompt, x_sample, state_pool, cache_k, cache_v, g_pool, w_pool, s_pool,
              g_kv, w_kv, g_attn, w_q, w_o, rel_bias, g_ffn, w_group, b_group, w_router,
              b_router, w_gate, w_up, w_down, g_final):
    y_prompt, pool_prompt, k_prompt, v_prompt = trunk(
        x_prompt, None, None, None, 0, g_pool, w_pool, s_pool, g_kv, w_kv, g_attn, w_q, w_o,
        rel_bias, g_ffn, w_group, b_group, w_router, b_router, w_gate, w_up, w_down, g_final)
    y_sample, pool_sample, k_sample, v_sample = trunk(
        x_sample, state_pool, cache_k, cache_v, PAST_LEN, g_pool, w_pool, s_pool, g_kv, w_kv,
        g_attn, w_q, w_o, rel_bias, g_ffn, w_group, b_group, w_router, b_router, w_gate, w_up,
        w_down, g_final)
    keep = min(LEFT_LEN, x_prompt.shape[1])
    return (y_prompt, y_sample, pool_prompt, pool_sample,
            k_prompt[:, -keep:], v_prompt[:, -keep:], k_sample, v_sample)
```

```python
import functools

import jax
import jax.numpy as jnp
from jax import lax
from jax.experimental import pallas as pl
from jax.experimental.pallas import tpu as pltpu

RMS_EPS = 1e-6
NEG_INF = -1e30
CHUNK = 64
LEFT_CHUNKS = 8
LEFT_LEN = LEFT_CHUNKS * CHUNK
POOL_WINDOWS = (2, 4, 8, 16)
POOL_HIST = max(POOL_WINDOWS) - 1
HALO = POOL_HIST + 1
PAST_LEN = 1024
LANES = 128

F32 = jnp.float32
BF16 = jnp.bfloat16


def _pick(n, cands):
    for c in cands:
        if n % c == 0:
            return c
    raise ValueError(f"no tile for {n} in {cands}")


def _params(sem, vmem_mib=None):
    return pltpu.CompilerParams(
        dimension_semantics=sem,
        vmem_limit_bytes=None if vmem_mib is None else vmem_mib * 1024 * 1024)


def _rms(x, g):
    ms = jnp.mean(x * x, axis=-1, keepdims=True)
    return (x * lax.rsqrt(ms + RMS_EPS)) * g


def _pool_kernel(*refs, ts, dg, pos0, aliased):
    if aliased:
        x_ref, halo0_ref, g_ref, w_ref, s_ref, _, out_ref, hist_ref, halo_sc = refs
    else:
        x_ref, halo0_ref, g_ref, w_ref, s_ref, out_ref, hist_ref, halo_sc = refs
    s = pl.program_id(1)
    ns = pl.num_programs(1)

    @pl.when(s == 0)
    def _():
        halo_sc[...] = halo0_ref[0]

    x = x_ref[...]
    h = _rms(x, g_ref[...])
    halo = halo_sc[...]
    pos = lax.broadcasted_iota(jnp.int32, (ts, 1), 0) + (s * ts + pos0)
    for gi, w in enumerate(POOL_WINDOWS):
        lo, hi = gi * dg, (gi + 1) * dg
        hg = h[:, lo:hi]
        acc = jnp.concatenate([halo[:, lo:hi], hg], axis=0)
        k = 1
        while k < w:
            acc = acc + pltpu.roll(acc, k, 0)
            k *= 2
        cnt = jnp.minimum(pos + 1, w).astype(F32)
        pooled = acc[HALO:, :] * (1.0 / cnt) - hg
        mix = jnp.dot(pooled.astype(BF16), w_ref[gi], preferred_element_type=F32) * s_ref[:, lo:hi]
        out_ref[:, lo:hi] = x[:, lo:hi] + mix
    halo_sc[...] = h[ts - HALO:, :]

    @pl.when(s == ns - 1)
    def _():
        hist_ref[0] = halo_sc[1:HALO, :]


def _pool_layer(x_in, in_off, out_buf, out_off, n_tok_total, batch, seq, halo0, g, w_bf, s, pos0):
    d = x_in.shape[1]
    dg = d // len(POOL_WINDOWS)
    ts = _pick(seq, (256, 128, 64))
    ns = seq // ts
    assert in_off % ts == 0 and out_off % ts == 0
    ib, ob = in_off // ts, out_off // ts
    aliased = out_buf is not None
    halo_b = halo0.shape[0]
    in_specs = [
        pl.BlockSpec((ts, d), lambda b, i: (ib + b * ns + i, 0)),
        pl.BlockSpec((1, HALO, d), (lambda b, i: (b, 0, 0)) if halo_b > 1 else (lambda b, i: (0, 0, 0))),
        pl.BlockSpec((1, d), lambda b, i: (0, 0)),
        pl.BlockSpec((len(POOL_WINDOWS), dg, dg), lambda b, i: (0, 0, 0)),
        pl.BlockSpec((1, d), lambda b, i: (0, 0)),
    ]
    args = [x_in, halo0, g.reshape(1, d), w_bf, s.reshape(1, d)]
    if aliased:
        in_specs.append(pl.BlockSpec(memory_space=pl.ANY))
        args.append(out_buf)
    return pl.pallas_call(
        functools.partial(_pool_kernel, ts=ts, dg=dg, pos0=pos0, aliased=aliased),
        grid=(batch, ns),
        in_specs=in_specs,
        out_specs=[
            pl.BlockSpec((ts, d), lambda b, i: (ob + b * ns + i, 0)),
            pl.BlockSpec((1, POOL_HIST, d), lambda b, i: (b, 0, 0)),
        ],
        out_shape=[
            jax.ShapeDtypeStruct((n_tok_total, d), F32),
            jax.ShapeDtypeStruct((batch, POOL_HIST, d), F32),
        ],
        scratch_shapes=[pltpu.VMEM((HALO, d), F32)],
        input_output_aliases={len(args) - 1: 0} if aliased else {},
        compiler_params=_params(("arbitrary", "arbitrary"), 48),
        name="pool_layer",
    )(*args)


def _router_kernel(x_ref, g_ref, w_ref, b_ref, route_ref, cnt_ref, carry_sc, *, tm, n_groups, per_group):
    i = pl.program_id(0)

    @pl.when(i == 0)
    def _():
        carry_sc[...] = jnp.zeros_like(carry_sc)

    h = _rms(x_ref[...], g_ref[...])
    lg = jnp.dot(h.astype(BF16), w_ref[...], preferred_element_type=F32) + b_ref[...]
    lane = lax.broadcasted_iota(jnp.int32, (tm, LANES), 1).astype(F32)
    big = float(LANES)

    def first_max(vals, mask):
        v = jnp.max(jnp.where(mask, vals, -jnp.inf), axis=-1, keepdims=True)
        idx = jnp.min(jnp.where(mask & (vals == v), lane, big), axis=-1, keepdims=True)
        return v, idx

    gmask = lane < n_groups
    gmax, gsel = first_max(lg, gmask)
    gsum = jnp.sum(jnp.where(gmask, jnp.exp(lg - gmax), 0.0), axis=-1, keepdims=True)
    g_prob = 1.0 / gsum
    lo = n_groups + gsel * per_group
    emask = (lane >= lo) & (lane < lo + per_group)
    v1, i1 = first_max(lg, emask)
    v2, i2 = first_max(lg, emask & (lane != i1))
    e21 = jnp.exp(v2 - v1)
    den = 1.0 + e21
    gate1 = (1.0 / den) * g_prob
    gate2 = (e21 / den) * g_prob

    sel1 = lane == i1
    sel2 = lane == i2
    onehot = jnp.where(sel1 | sel2, 1.0, 0.0)
    r_i = lax.broadcasted_iota(jnp.int32, (tm, tm), 0)
    c_i = lax.broadcasted_iota(jnp.int32, (tm, tm), 1)
    lower = jnp.where(c_i < r_i, 1.0, 0.0).astype(BF16)
    before = jnp.dot(lower, onehot.astype(BF16), preferred_element_type=F32) + carry_sc[...]
    rank1 = jnp.sum(jnp.where(sel1, before, 0.0), axis=-1, keepdims=True)
    rank2 = jnp.sum(jnp.where(sel2, before, 0.0), axis=-1, keepdims=True)
    carry = carry_sc[...] + jnp.sum(onehot, axis=0, keepdims=True)
    carry_sc[...] = carry
    cnt_ref[...] = carry

    slab = jnp.where(lane == 0, gate1, 0.0)
    slab = jnp.where(lane == 1, gate2, slab)
    slab = jnp.where(lane == 2, i1 - n_groups, slab)
    slab = jnp.where(lane == 3, i2 - n_groups, slab)
    slab = jnp.where(lane == 4, rank1, slab)
    slab = jnp.where(lane == 5, rank2, slab)
    route_ref[...] = slab


def _route(x, g, w_rt, b_rt, n_groups, per_group):
    t, d = x.shape
    tm = _pick(t, (512, 256, 128))
    return pl.pallas_call(
        functools.partial(_router_kernel, tm=tm, n_groups=n_groups, per_group=per_group),
        grid=(t // tm,),
        in_specs=[
            pl.BlockSpec((tm, d), lambda i: (i, 0)),
            pl.BlockSpec((1, d), lambda i: (0, 0)),
            pl.BlockSpec((d, LANES), lambda i: (0, 0)),
            pl.BlockSpec((1, LANES), lambda i: (0, 0)),
        ],
        out_specs=[
            pl.BlockSpec((tm, LANES), lambda i: (i, 0)),
            pl.BlockSpec((1, LANES), lambda i: (0, 0)),
        ],
        out_shape=[
            jax.ShapeDtypeStruct((t, LANES), F32),
            jax.ShapeDtypeStruct((1, LANES), F32),
        ],
        scratch_shapes=[pltpu.VMEM((1, LANES), F32)],
        compiler_params=_params(("arbitrary",), 48),
        name="moe_route",
    )(x, g.reshape(1, d), w_rt, b_rt)


def _row_gather(src_hbm, idx_ref, dst, sem, n):
    def body(r, c):
        pltpu.make_async_copy(src_hbm.at[pl.ds(idx_ref[0, 0, r], 1), :], dst.at[pl.ds(r, 1), :], sem).start()
        return c
    lax.fori_loop(0, n, body, 0)


def _row_gather_wait(src_hbm, dst, sem, n):
    def body(r, c):
        pltpu.make_async_copy(src_hbm.at[pl.ds(0, 1), :], dst.at[pl.ds(r, 1), :], sem).wait()
        return c
    lax.fori_loop(0, n, body, 0)


def _moe_kernel(be_ref, nv_ref, tokc_ref, tokn_ref, x_hbm, g_ref, wg_ref, wu_ref, wd_ref, ys_ref, xbuf, sem, *, tb):
    b = pl.program_id(0)
    nvalid = nv_ref[0]
    slot = lax.rem(b, 2)

    @pl.when((b == 0) & (nvalid > 0))
    def _():
        _row_gather(x_hbm, tokc_ref, xbuf.at[0], sem.at[0], tb)

    @pl.when(b + 1 < nvalid)
    def _():
        _row_gather(x_hbm, tokn_ref, xbuf.at[1 - slot], sem.at[1 - slot], tb)

    @pl.when(b < nvalid)
    def _():
        _row_gather_wait(x_hbm, xbuf.at[slot], sem.at[slot], tb)
        hn = _rms(xbuf[slot], g_ref[...]).astype(BF16)
        a = jnp.dot(hn, wg_ref[...], preferred_element_type=F32)
        u = jnp.dot(hn, wu_ref[...], preferred_element_type=F32)
        act = (a * (1.0 / (1.0 + jnp.exp(-a)))) * u
        ys_ref[...] = jnp.dot(act.astype(BF16), wd_ref[...], preferred_element_type=F32)

    @pl.when(b >= nvalid)
    def _():
        ys_ref[...] = jnp.zeros_like(ys_ref)


def _moe_mlp(x, g, wg, wu, wd, layer, blk_expert, nvalid, slot_tok, tb):
    t, d = x.shape
    de = wg.shape[-1]
    nb = blk_expert.shape[0]
    tok3 = slot_tok.reshape(nb, 1, tb)
    grid_spec = pltpu.PrefetchScalarGridSpec(
        num_scalar_prefetch=2,
        grid=(nb,),
        in_specs=[
            pl.BlockSpec((1, 1, tb), lambda b, be, nv: (b, 0, 0), memory_space=pltpu.SMEM),
            pl.BlockSpec((1, 1, tb), lambda b, be, nv: (jnp.minimum(b + 1, nb - 1), 0, 0), memory_space=pltpu.SMEM),
            pl.BlockSpec(memory_space=pl.ANY),
            pl.BlockSpec((1, d), lambda b, be, nv: (0, 0)),
            pl.BlockSpec((None, None, d, de), lambda b, be, nv: (layer, be[b], 0, 0)),
            pl.BlockSpec((None, None, d, de), lambda b, be, nv: (layer, be[b], 0, 0)),
            pl.BlockSpec((None, None, de, d), lambda b, be, nv: (layer, be[b], 0, 0)),
        ],
        out_specs=pl.BlockSpec((tb, d), lambda b, be, nv: (b, 0)),
        scratch_shapes=[pltpu.VMEM((2, tb, d), F32), pltpu.SemaphoreType.DMA((2,))],
    )
    return pl.pallas_call(
        functools.partial(_moe_kernel, tb=tb),
        grid_spec=grid_spec,
        out_shape=jax.ShapeDtypeStruct((nb * tb, d), F32),
        compiler_params=_params(("arbitrary",), 48),
        name="moe_mlp",
    )(blk_expert, nvalid, tok3, tok3, x, g.reshape(1, d), wg, wu, wd)


def _combine_kernel(*refs, tm, final):
    if final:
        dc_ref, dn_ref, x_ref, route_ref, ys_hbm, gf_ref, out_ref, ybuf, sem = refs
    else:
        dc_ref, dn_ref, x_ref, route_ref, ys_hbm, out_ref, ybuf, sem = refs
    i = pl.program_id(0)
    n = pl.num_programs(0)
    slot = lax.rem(i, 2)

    @pl.when(i == 0)
    def _():
        _row_gather(ys_hbm, dc_ref, ybuf.at[0], sem.at[0], 2 * tm)

    @pl.when(i + 1 < n)
    def _():
        _row_gather(ys_hbm, dn_ref, ybuf.at[1 - slot], sem.at[1 - slot], 2 * tm)

    _row_gather_wait(ys_hbm, ybuf.at[slot], sem.at[slot], 2 * tm)
    y0 = ybuf[slot, 0:tm, :]
    y1 = ybuf[slot, tm:2 * tm, :]
    out = x_ref[...] + (route_ref[:, 0:1] * y0 + route_ref[:, 1:2] * y1)
    if final:
        out = _rms(out, gf_ref[...])
    out_ref[...] = out


def _combine(x, route, ys, dest0, dest1, g_final=None):
    t, d = x.shape
    tm = _pick(t, (128, 64))
    n = t // tm
    dest = jnp.concatenate([dest0.reshape(n, 1, tm), dest1.reshape(n, 1, tm)], axis=2)
    final = g_final is not None
    in_specs = [
        pl.BlockSpec((1, 1, 2 * tm), lambda i: (i, 0, 0), memory_space=pltpu.SMEM),
        pl.BlockSpec((1, 1, 2 * tm), lambda i: (jnp.minimum(i + 1, n - 1), 0, 0), memory_space=pltpu.SMEM),
        pl.BlockSpec((tm, d), lambda i: (i, 0)),
        pl.BlockSpec((tm, LANES), lambda i: (i, 0)),
        pl.BlockSpec(memory_space=pl.ANY),
    ]
    args = [dest, dest, x, route, ys]
    if final:
        in_specs.append(pl.BlockSpec((1, d), lambda i: (0, 0)))
        args.append(g_final.reshape(1, d))
    return pl.pallas_call(
        functools.partial(_combine_kernel, tm=tm, final=final),
        grid=(n,),
        in_specs=in_specs,
        out_specs=pl.BlockSpec((tm, d), lambda i: (i, 0)),
        out_shape=jax.ShapeDtypeStruct((t, d), F32),
        scratch_shapes=[pltpu.VMEM((2, 2 * tm, d), F32), pltpu.SemaphoreType.DMA((2,))],
        compiler_params=_params(("arbitrary",), 48),
        name="moe_combine",
    )(*args)


def _hier_moe(x, layer, p, tb, g_final=None):
    t, d = x.shape
    n_groups, n_experts = p["n_groups"], p["n_experts"]
    per_group = n_experts // n_groups
    route, cnt = _route(x, p["g_ffn"][layer], p["w_rt"][layer], p["b_rt"][layer], n_groups, per_group)

    e0 = route[:, 2].astype(jnp.int32)
    e1 = route[:, 3].astype(jnp.int32)
    r0 = route[:, 4].astype(jnp.int32)
    r1 = route[:, 5].astype(jnp.int32)
    counts = cnt[0, n_groups:n_groups + n_experts].astype(jnp.int32)
    padded = (counts + tb - 1) // tb * tb
    pend = jnp.cumsum(padded)
    pstart = pend - padded
    dest0 = pstart[e0] + r0
    dest1 = pstart[e1] + r1
    n_blocks = -(-(2 * t + n_experts * (tb - 1)) // tb)
    tok = jnp.arange(t, dtype=jnp.int32)
    slot_tok = jnp.zeros((n_blocks * tb,), jnp.int32).at[dest0].set(tok).at[dest1].set(tok)
    blk_start = jnp.arange(n_blocks, dtype=jnp.int32) * tb
    blk_expert = jnp.minimum(jnp.searchsorted(pend, blk_start, side="right"), n_experts - 1).astype(jnp.int32)
    nvalid = (pend[-1:] // tb).astype(jnp.int32)

    ys = _moe_mlp(x, p["g_ffn"][layer], p["w_gate"], p["w_up"], p["w_down"], layer, blk_expert, nvalid, slot_tok, tb)
    return _combine(x, route, ys, dest0, dest1, g_final)


def _rms_proj_kernel(x_ref, g_ref, w_ref, *refs):
    out_refs, hn_sc = refs[:-1], refs[-1]

    @pl.when(pl.program_id(1) == 0)
    def _():
        hn_sc[...] = _rms(x_ref[...], g_ref[...]).astype(BF16)

    res = jnp.dot(hn_sc[...], w_ref[...], preferred_element_type=F32)
    for o in out_refs:
        o[...] = res.astype(o.dtype)


def _rms_proj(x, g, w_bf, out_dtypes):
    t, d = x.shape
    n = w_bf.shape[1]
    tm = _pick(t, (1024, 512, 256, 128))
    tn = _pick(n, (512, 256, 128))
    return pl.pallas_call(
        _rms_proj_kernel,
        grid=(t // tm, n // tn),
        in_specs=[
            pl.BlockSpec((tm, d), lambda i, j: (i, 0)),
            pl.BlockSpec((1, d), lambda i, j: (0, 0)),
            pl.BlockSpec((d, tn), lambda i, j: (0, j)),
        ],
        out_specs=[pl.BlockSpec((tm, tn), lambda i, j: (i, j)) for _ in out_dtypes],
        out_shape=[jax.ShapeDtypeStruct((t, n), dt) for dt in out_dtypes],
        scratch_shapes=[pltpu.VMEM((tm, d), BF16)],
        compiler_params=_params(("arbitrary", "arbitrary"), 48),
        name="rms_proj",
    )(x, g.reshape(1, d), w_bf)


def _proj_res_kernel(o_ref, w_ref, x_ref, out_ref):
    out_ref[...] = x_ref[...] + jnp.dot(o_ref[...], w_ref[...], preferred_element_type=F32)


def _proj_res(o_bf, w_bf, x):
    t, d = x.shape
    k = o_bf.shape[1]
    tm = _pick(t, (1024, 512, 256, 128))
    tn = _pick(d, (512, 256, 128))
    return pl.pallas_call(
        _proj_res_kernel,
        grid=(t // tm, d // tn),
        in_specs=[
            pl.BlockSpec((tm, k), lambda i, j: (i, 0)),
            pl.BlockSpec((k, tn), lambda i, j: (0, j)),
            pl.BlockSpec((tm, tn), lambda i, j: (i, j)),
        ],
        out_specs=pl.BlockSpec((tm, tn), lambda i, j: (i, j)),
        out_shape=jax.ShapeDtypeStruct((t, d), F32),
        compiler_params=_params(("parallel", "parallel"), 48),
        name="proj_res",
    )(o_bf, w_bf, x)


def _attn_kernel(*refs, n_seg, n_heads, hd, scale, clamp_mask):
    q_ref = refs[0]
    kv_refs = refs[1:1 + 2 * n_seg]
    bias_ref = refs[1 + 2 * n_seg]
    o_ref = refs[-1]
    i = pl.program_id(1)

    def seg(ref, sl):
        v = ref[0, :, sl] if len(ref.shape) == 3 else ref[:, sl]
        return v.astype(BF16)

    for h in range(n_heads):
        sl = slice(h * hd, (h + 1) * hd)
        qh = q_ref[:, sl]
        scores = []
        off = 0
        for si in range(n_seg):
            kh = seg(kv_refs[2 * si], sl)
            nk = kh.shape[0]
            s = lax.dot_general(qh, kh, (((1,), (1,)), ((), ())), preferred_element_type=F32) * scale
            s = s + bias_ref[h, :, off:off + nk]
            if clamp_mask and si < n_seg - 1:
                s = jnp.where(i >= n_seg - 1 - si, s, NEG_INF)
            scores.append(s)
            off += nk
        m = functools.reduce(jnp.maximum, [jnp.max(s, axis=-1, keepdims=True) for s in scores])
        ps = [jnp.exp(s - m) for s in scores]
        l = functools.reduce(lambda a, b: a + b, [jnp.sum(p, axis=-1, keepdims=True) for p in ps])
        acc = None
        for si in range(n_seg):
            pv = jnp.dot(ps[si].astype(BF16), seg(kv_refs[2 * si + 1], sl), preferred_element_type=F32)
            acc = pv if acc is None else acc + pv
        o_ref[:, sl] = (acc / l).astype(o_ref.dtype)


def _band_bias(table, qb, n_keys, left):
    clip = (table.shape[-1] - 1) // 2
    i = jnp.arange(qb, dtype=jnp.int32)[:, None]
    j = jnp.arange(n_keys, dtype=jnp.int32)[None, :]
    rel = jnp.clip(i - (j - left), -clip, clip) + clip
    kpos = j - left
    cq = (i // CHUNK) * CHUNK
    inband = (kpos >= cq - LEFT_LEN) & (kpos < cq + CHUNK)
    return jnp.where(inband[None], table[:, rel].astype(F32), NEG_INF)


def _attn_prompt(q, kv, table, batch, seq, n_heads, n_tok_total):
    d = q.shape[1]
    hd = d // n_heads
    qb = _pick(seq, (256, 128, 64))
    nq = seq // qb
    n_seg = LEFT_LEN // qb + 1
    bias = _band_bias(table, qb, n_seg * qb, LEFT_LEN)

    def kv_spec(si, col):
        back = n_seg - 1 - si
        return pl.BlockSpec((qb, d), lambda b, i: (b * nq + jnp.maximum(i - back, 0), col))

    in_specs = [pl.BlockSpec((qb, d), lambda b, i: (b * nq + i, 0))]
    args = [q]
    for si in range(n_seg):
        in_specs += [kv_spec(si, 0), kv_spec(si, 1)]
        args += [kv, kv]
    in_specs.append(pl.BlockSpec((n_heads, qb, n_seg * qb), lambda b, i: (0, 0, 0)))
    args.append(bias)
    return pl.pallas_call(
        functools.partial(_attn_kernel, n_seg=n_seg, n_heads=n_heads, hd=hd, scale=hd ** -0.5,
                          clamp_mask=True),
        grid=(batch, nq),
        in_specs=in_specs,
        out_specs=pl.BlockSpec((qb, d), lambda b, i: (b * nq + i, 0)),
        out_shape=jax.ShapeDtypeStruct((n_tok_total, d), BF16),
        compiler_params=_params(("parallel", "arbitrary"), 56),
        name="attn_prompt",
    )(*args)


def _attn_sample_kernel(*refs, **kw):
    _attn_kernel(*refs[:-2], refs[-1], **kw)


def _attn_sample(q, kv, cache_k, cache_v, table, row0, batch, seq, n_heads, o_buf):
    d = q.shape[1]
    hd = d // n_heads
    past = cache_k.shape[1]
    assert row0 % seq == 0
    rb = row0 // seq
    assert seq <= CHUNK and past == LEFT_LEN
    bias = _band_bias(table, seq, past + seq, past)
    in_specs = [
        pl.BlockSpec((seq, d), lambda b, i: (rb + b, 0)),
        pl.BlockSpec((1, past, d), lambda b, i: (b, 0, 0)),
        pl.BlockSpec((1, past, d), lambda b, i: (b, 0, 0)),
        pl.BlockSpec((seq, d), lambda b, i: (rb + b, 0)),
        pl.BlockSpec((seq, d), lambda b, i: (rb + b, 1)),
        pl.BlockSpec((n_heads, seq, past + seq), lambda b, i: (0, 0, 0)),
        pl.BlockSpec(memory_space=pl.ANY),
    ]
    return pl.pallas_call(
        functools.partial(_attn_sample_kernel, n_seg=2, n_heads=n_heads, hd=hd, scale=hd ** -0.5,
                          clamp_mask=False),
        grid=(batch, 1),
        in_specs=in_specs,
        out_specs=pl.BlockSpec((seq, d), lambda b, i: (rb + b, 0)),
        out_shape=jax.ShapeDtypeStruct(o_buf.shape, BF16),
        input_output_aliases={6: 0},
        compiler_params=_params(("parallel", "arbitrary"), 56),
        name="attn_sample",
    )(q, cache_k, cache_v, kv, kv, bias, o_buf)


def kernel(x_prompt, x_sample, state_pool, cache_k, cache_v, g_pool, w_pool, s_pool, g_kv, w_kv, g_attn, w_q, w_o, rel_bias, g_ffn, w_group, b_group, w_router, b_router, w_gate, w_up, w_down, g_final):
    bp, lp, d = x_prompt.shape
    bs, ls, _ = x_sample.shape
    depth = g_ffn.shape[0]
    n_pool = g_pool.shape[0]
    n_heads, hd = cache_k.shape[2], cache_k.shape[3]
    n_groups, n_experts = w_group.shape[-1], w_router.shape[-1]
    tp, tsm = bp * lp, bs * ls
    t = tp + tsm
    tb = 256

    w_rt = jnp.concatenate([w_group, w_router], axis=-1)
    w_rt = jnp.pad(w_rt, ((0, 0), (0, 0), (0, LANES - w_rt.shape[-1]))).astype(BF16)
    b_rt = jnp.concatenate([b_group, b_router], axis=-1)
    b_rt = jnp.pad(b_rt, ((0, 0), (0, LANES - b_rt.shape[-1])))[:, None, :]
    moe_p = dict(n_groups=n_groups, n_experts=n_experts, g_ffn=g_ffn, w_rt=w_rt, b_rt=b_rt,
                 w_gate=w_gate.astype(BF16), w_up=w_up.astype(BF16), w_down=w_down.astype(BF16))
    w_pool_bf = w_pool.astype(BF16)
    w_kv_bf = w_kv.astype(BF16)
    w_q_bf = w_q.astype(BF16)
    w_o_bf = w_o.astype(BF16)

    zero_halo = jnp.zeros((1, HALO, d), F32)
    samp_halo = jnp.pad(state_pool, ((0, 0), (0, 0), (1, 0), (0, 0)))
    ck = cache_k.reshape(bs, cache_k.shape[1], d)
    cv = cache_v.reshape(bs, cache_v.shape[1], d)

    x = None
    hist_p, hist_s = [], []
    kv_f32 = kv_bf = None
    for layer in range(depth):
        if layer < n_pool:
            if layer == 0:
                src_p, off_p, src_s, off_s = x_prompt.reshape(tp, d), 0, x_sample.reshape(tsm, d), 0
            else:
                src_p, off_p, src_s, off_s = x, 0, x, tp
            buf, hp = _pool_layer(src_p, off_p, None, 0, t, bp, lp, zero_halo, g_pool[layer], w_pool_bf[layer],
                                  s_pool[layer], 0)
            x, hs = _pool_layer(src_s, off_s, buf, tp, t, bs, ls, samp_halo[layer], g_pool[layer], w_pool_bf[layer],
                                s_pool[layer], PAST_LEN)
            hist_p.append(hp)
            hist_s.append(hs)
        else:
            j = layer - n_pool
            if j == 0:
                kv_f32, kv_bf = _rms_proj(x, g_kv, w_kv_bf, (F32, BF16))
            (q,) = _rms_proj(x, g_attn[j], w_q_bf[j], (BF16,))
            o = _attn_prompt(q, kv_bf, rel_bias[j], bp, lp, n_heads, t)
            o = _attn_sample(q, kv_bf, ck, cv, rel_bias[j], tp, bs, ls, n_heads, o)
            x = _proj_res(o, w_o_bf[j], x)
        x = _hier_moe(x, layer, moe_p, tb, g_final if layer == depth - 1 else None)

    y_prompt = x[:tp].reshape(bp, lp, d)
    y_sample = x[tp:].reshape(bs, ls, d)
    keep = min(LEFT_LEN, lp)
    kvp = kv_f32[:tp].reshape(bp, lp, 2 * d)[:, lp - keep:]
    kvs = kv_f32[tp:].reshape(bs, ls, 2 * d)
    return (y_prompt, y_sample, jnp.stack(hist_p, axis=0), jnp.stack(hist_s, axis=0),
            kvp[..., :d].reshape(bp, keep, n_heads, hd), kvp[..., d:].reshape(bp, keep, n_heads, hd),
            kvs[..., :d].reshape(bs, ls, n_heads, hd), kvs[..., d:].reshape(bs, ls, n_heads, hd))
```

```python
import functools

import jax
import jax.numpy as jnp
from jax import lax
from jax.experimental import pallas as pl
from jax.experimental.pallas import tpu as pltpu

RMS_EPS = 1e-6
NEG_INF = -1e30
CHUNK = 64
LEFT_CHUNKS = 8
LEFT_LEN = LEFT_CHUNKS * CHUNK
POOL_WINDOWS = (2, 4, 8, 16)
POOL_HIST = max(POOL_WINDOWS) - 1
HALO = POOL_HIST + 1
PAST_LEN = 1024
LANES = 128

F32 = jnp.float32
BF16 = jnp.bfloat16


def _pick(n, cands):
    for c in cands:
        if n % c == 0:
            return c
    raise ValueError(f"no tile for {n} in {cands}")


def _params(sem, vmem_mib=None):
    return pltpu.CompilerParams(
        dimension_semantics=sem,
        vmem_limit_bytes=None if vmem_mib is None else vmem_mib * 1024 * 1024)


def _rms(x, g):
    ms = jnp.mean(x * x, axis=-1, keepdims=True)
    return (x * lax.rsqrt(ms + RMS_EPS)) * g


def _pool_kernel(*refs, ts, dg, pos0, aliased):
    if aliased:
        x_ref, halo0_ref, g_ref, w_ref, s_ref, _, out_ref, hist_ref, halo_sc = refs
    else:
        x_ref, halo0_ref, g_ref, w_ref, s_ref, out_ref, hist_ref, halo_sc = refs
    s = pl.program_id(1)
    ns = pl.num_programs(1)

    @pl.when(s == 0)
    def _():
        halo_sc[...] = halo0_ref[0]

    x = x_ref[...]
    h = _rms(x, g_ref[...])
    halo = halo_sc[...]
    pos = lax.broadcasted_iota(jnp.int32, (ts, 1), 0) + (s * ts + pos0)
    for gi, w in enumerate(POOL_WINDOWS):
        lo, hi = gi * dg, (gi + 1) * dg
        hg = h[:, lo:hi]
        acc = jnp.concatenate([halo[:, lo:hi], hg], axis=0)
        k = 1
        while k < w:
            acc = acc + pltpu.roll(acc, k, 0)
            k *= 2
        cnt = jnp.minimum(pos + 1, w).astype(F32)
        pooled = acc[HALO:, :] * (1.0 / cnt) - hg
        mix = jnp.dot(pooled.astype(BF16), w_ref[gi], preferred_element_type=F32) * s_ref[:, lo:hi]
        out_ref[:, lo:hi] = x[:, lo:hi] + mix
    halo_sc[...] = h[ts - HALO:, :]

    @pl.when(s == ns - 1)
    def _():
        hist_ref[0] = halo_sc[1:HALO, :]


def _pool_layer(x_in, in_off, out_buf, out_off, n_tok_total, batch, seq, halo0, g, w_bf, s, pos0):
    d = x_in.shape[1]
    dg = d // len(POOL_WINDOWS)
    ts = _pick(seq, (256, 128, 64))
    ns = seq // ts
    assert in_off % ts == 0 and out_off % ts == 0
    ib, ob = in_off // ts, out_off // ts
    aliased = out_buf is not None
    halo_b = halo0.shape[0]
    in_specs = [
        pl.BlockSpec((ts, d), lambda b, i: (ib + b * ns + i, 0)),
        pl.BlockSpec((1, HALO, d), (lambda b, i: (b, 0, 0)) if halo_b > 1 else (lambda b, i: (0, 0, 0))),
        pl.BlockSpec((1, d), lambda b, i: (0, 0)),
        pl.BlockSpec((len(POOL_WINDOWS), dg, dg), lambda b, i: (0, 0, 0)),
        pl.BlockSpec((1, d), lambda b, i: (0, 0)),
    ]
    args = [x_in, halo0, g.reshape(1, d), w_bf, s.reshape(1, d)]
    if aliased:
        in_specs.append(pl.BlockSpec(memory_space=pl.ANY))
        args.append(out_buf)
    return pl.pallas_call(
        functools.partial(_pool_kernel, ts=ts, dg=dg, pos0=pos0, aliased=aliased),
        grid=(batch, ns),
        in_specs=in_specs,
        out_specs=[
            pl.BlockSpec((ts, d), lambda b, i: (ob + b * ns + i, 0)),
            pl.BlockSpec((1, POOL_HIST, d), lambda b, i: (b, 0, 0)),
        ],
        out_shape=[
            jax.ShapeDtypeStruct((n_tok_total, d), F32),
            jax.ShapeDtypeStruct((batch, POOL_HIST, d), F32),
        ],
        scratch_shapes=[pltpu.VMEM((HALO, d), F32)],
        input_output_aliases={len(args) - 1: 0} if aliased else {},
        compiler_params=_params(("arbitrary", "arbitrary"), 48),
        name="pool_layer",
    )(*args)


def _router_kernel(x_ref, g_ref, w_ref, b_ref, route_ref, cnt_ref, carry_sc, *, tm, n_groups, per_group):
    i = pl.program_id(0)

    @pl.when(i == 0)
    def _():
        carry_sc[...] = jnp.zeros_like(carry_sc)

    h = _rms(x_ref[...], g_ref[...])
    lg = jnp.dot(h.astype(BF16), w_ref[...], preferred_element_type=F32) + b_ref[...]
    lane = lax.broadcasted_iota(jnp.int32, (tm, LANES), 1).astype(F32)
    big = float(LANES)

    def first_max(vals, mask):
        v = jnp.max(jnp.where(mask, vals, -jnp.inf), axis=-1, keepdims=True)
        idx = jnp.min(jnp.where(mask & (vals == v), lane, big), axis=-1, keepdims=True)
        return v, idx

    gmask = lane < n_groups
    gmax, gsel = first_max(lg, gmask)
    gsum = jnp.sum(jnp.where(gmask, jnp.exp(lg - gmax), 0.0), axis=-1, keepdims=True)
    g_prob = 1.0 / gsum
    lo = n_groups + gsel * per_group
    emask = (lane >= lo) & (lane < lo + per_group)
    v1, i1 = first_max(lg, emask)
    v2, i2 = first_max(lg, emask & (lane != i1))
    e21 = jnp.exp(v2 - v1)
    den = 1.0 + e21
    gate1 = (1.0 / den) * g_prob
    gate2 = (e21 / den) * g_prob

    first_lo = i1 < i2
    a = jnp.minimum(i1, i2) - lo
    b = jnp.maximum(i1, i2) - lo
    gate_a = jnp.where(first_lo, gate1, gate2)
    gate_b = jnp.where(first_lo, gate2, gate1)
    n_pairs = per_group * (per_group - 1) // 2
    key = gsel * n_pairs + a * (2 * per_group - 1 - a) * 0.5 + (b - a - 1.0)
    sel = lane == key
    onehot = jnp.where(sel, 1.0, 0.0)
    r_i = lax.broadcasted_iota(jnp.int32, (tm, tm), 0)
    c_i = lax.broadcasted_iota(jnp.int32, (tm, tm), 1)
    lower = jnp.where(c_i < r_i, 1.0, 0.0).astype(BF16)
    before = jnp.dot(lower, onehot.astype(BF16), preferred_element_type=F32) + carry_sc[...]
    rank = jnp.sum(jnp.where(sel, before, 0.0), axis=-1, keepdims=True)
    carry = carry_sc[...] + jnp.sum(onehot, axis=0, keepdims=True)
    carry_sc[...] = carry
    cnt_ref[...] = carry

    slab = jnp.where(lane == 0, gate_a, 0.0)
    slab = jnp.where(lane == 1, gate_b, slab)
    slab = jnp.where(lane == 2, key, slab)
    slab = jnp.where(lane == 3, rank, slab)
    route_ref[...] = slab


def _route(x, t, g, w_rt, b_rt, n_groups, per_group):
    d = x.shape[1]
    tm = _pick(t, (512, 256, 128))
    return pl.pallas_call(
        functools.partial(_router_kernel, tm=tm, n_groups=n_groups, per_group=per_group),
        grid=(t // tm,),
        in_specs=[
            pl.BlockSpec((tm, d), lambda i: (i, 0)),
            pl.BlockSpec((1, d), lambda i: (0, 0)),
            pl.BlockSpec((d, LANES), lambda i: (0, 0)),
            pl.BlockSpec((1, LANES), lambda i: (0, 0)),
        ],
        out_specs=[
            pl.BlockSpec((tm, LANES), lambda i: (i, 0)),
            pl.BlockSpec((1, LANES), lambda i: (0, 0)),
        ],
        out_shape=[
            jax.ShapeDtypeStruct((t, LANES), F32),
            jax.ShapeDtypeStruct((1, LANES), F32),
        ],
        scratch_shapes=[pltpu.VMEM((1, LANES), F32)],
        compiler_params=_params(("arbitrary",), 48),
        name="moe_route",
    )(x, g.reshape(1, d), w_rt, b_rt)


def _moe_kernel(ea_ref, eb_ref, nv_ref, srcc_ref, srcn_ref, dst_ref, x_hbm, gate_ref, g_ref,
                wga_ref, wua_ref, wda_ref, wgb_ref, wub_ref, wdb_ref, *rest, tb, n_tok, final):
    if final:
        gf_ref, out_hbm, xbuf, obuf, gsem, ssem = rest
    else:
        out_hbm, xbuf, obuf, gsem, ssem = rest
    b = pl.program_id(0)
    nb = pl.num_programs(0)
    slot = lax.rem(b, 2)
    nv = nv_ref[b]
    nv_next = nv_ref[jnp.minimum(b + 1, nb - 1)]
    nv_prev2 = nv_ref[jnp.maximum(b - 2, 0)]
    nv_prev1 = nv_ref[jnp.maximum(b - 1, 0)]

    def gather(idx_ref, s):
        for r in range(tb):
            pltpu.make_async_copy(x_hbm.at[pl.ds(idx_ref[0, 0, r], 1), :], xbuf.at[s, pl.ds(r, 1), :],
                                  gsem.at[s]).start()

    def gather_wait(s):
        pltpu.make_async_copy(x_hbm.at[pl.ds(0, tb), :], xbuf.at[s], gsem.at[s]).wait()

    def scatter_wait(s):
        pltpu.make_async_copy(obuf.at[s], out_hbm.at[pl.ds(0, tb), :], ssem.at[s]).wait()

    @pl.when(b == 0)
    def _():
        obuf[...] = jnp.zeros_like(obuf)
        for s in range(2):
            spare = pltpu.make_async_copy(obuf.at[s], out_hbm.at[pl.ds(n_tok + s * tb, tb), :], ssem.at[s])
            spare.start()
            spare.wait()

    @pl.when((b == 0) & (nv > 0))
    def _():
        gather(srcc_ref, 0)

    @pl.when((b + 1 < nb) & (nv_next > 0))
    def _():
        gather(srcn_ref, 1 - slot)

    @pl.when((b >= 2) & (nv_prev2 > 0))
    def _():
        scatter_wait(slot)

    @pl.when(nv > 0)
    def _():
        gather_wait(slot)
        x = xbuf[slot]
        hn = _rms(x, g_ref[...]).astype(BF16)

        def expert(wg_ref, wu_ref, wd_ref):
            a = jnp.dot(hn, wg_ref[...], preferred_element_type=F32)
            u = jnp.dot(hn, wu_ref[...], preferred_element_type=F32)
            act = (a * (1.0 / (1.0 + jnp.exp(-a)))) * u
            return jnp.dot(act.astype(BF16), wd_ref[...], preferred_element_type=F32)

        ya = expert(wga_ref, wua_ref, wda_ref)
        yb = expert(wgb_ref, wub_ref, wdb_ref)
        out = x + (gate_ref[:, 0:1] * ya + gate_ref[:, 1:2] * yb)
        if final:
            out = _rms(out, gf_ref[...])
        obuf[slot] = out

        for r in range(tb):
            pltpu.make_async_copy(obuf.at[slot, pl.ds(r, 1), :], out_hbm.at[pl.ds(dst_ref[0, 0, r], 1), :],
                                  ssem.at[slot]).start()

    @pl.when(b == nb - 1)
    def _():
        @pl.when((b >= 1) & (nv_prev1 > 0))
        def _():
            scatter_wait(1 - slot)

        @pl.when(nv > 0)
        def _():
            scatter_wait(slot)


def _moe_mlp(x, t, g, wg, wu, wd, layer, blk_ea, blk_eb, blk_nv, slot_src, slot_dst, slot_gate, tb, g_final):
    d = x.shape[1]
    de = wg.shape[-1]
    nb = blk_nv.shape[0]
    src3 = slot_src.reshape(nb, 1, tb)
    dst3 = slot_dst.reshape(nb, 1, tb)
    final = g_final is not None

    def wspec(which, shape):
        return pl.BlockSpec((None, None) + shape, lambda b, ea, eb, nv: (layer, (ea, eb)[which][b], 0, 0))

    in_specs = [
        pl.BlockSpec((1, 1, tb), lambda b, ea, eb, nv: (b, 0, 0), memory_space=pltpu.SMEM),
        pl.BlockSpec((1, 1, tb), lambda b, ea, eb, nv: (jnp.minimum(b + 1, nb - 1), 0, 0), memory_space=pltpu.SMEM),
        pl.BlockSpec((1, 1, tb), lambda b, ea, eb, nv: (b, 0, 0), memory_space=pltpu.SMEM),
        pl.BlockSpec(memory_space=pl.ANY),
        pl.BlockSpec((tb, 2), lambda b, ea, eb, nv: (b, 0)),
        pl.BlockSpec((1, d), lambda b, ea, eb, nv: (0, 0)),
        wspec(0, (d, de)), wspec(0, (d, de)), wspec(0, (de, d)),
        wspec(1, (d, de)), wspec(1, (d, de)), wspec(1, (de, d)),
    ]
    args = [blk_ea, blk_eb, blk_nv, src3, src3, dst3, x, slot_gate, g.reshape(1, d), wg, wu, wd, wg, wu, wd]
    if final:
        in_specs.append(pl.BlockSpec((1, d), lambda b, ea, eb, nv: (0, 0)))
        args.append(g_final.reshape(1, d))
    grid_spec = pltpu.PrefetchScalarGridSpec(
        num_scalar_prefetch=3,
        grid=(nb,),
        in_specs=in_specs,
        out_specs=pl.BlockSpec(memory_space=pl.ANY),
        scratch_shapes=[pltpu.VMEM((2, tb, d), F32), pltpu.VMEM((2, tb, d), F32),
                        pltpu.SemaphoreType.DMA((2,)), pltpu.SemaphoreType.DMA((2,))],
    )
    return pl.pallas_call(
        functools.partial(_moe_kernel, tb=tb, n_tok=t, final=final),
        grid_spec=grid_spec,
        out_shape=jax.ShapeDtypeStruct((t + 2 * tb, d), F32),
        compiler_params=_params(("arbitrary",), 56),
        name="moe_mlp",
    )(*args)


def _pair_tables(n_groups, per_group):
    ea, eb = [], []
    for grp in range(n_groups):
        for a in range(per_group):
            for b in range(a + 1, per_group):
                ea.append(grp * per_group + a)
                eb.append(grp * per_group + b)
    return jnp.asarray(ea, jnp.int32), jnp.asarray(eb, jnp.int32)


def _hier_moe(x, t, layer, p, tb, g_final=None):
    n_groups, n_experts = p["n_groups"], p["n_experts"]
    per_group = n_experts // n_groups
    n_buckets = n_groups * (per_group * (per_group - 1) // 2)
    assert n_buckets <= LANES
    route, cnt = _route(x, t, p["g_ffn"][layer], p["w_rt"][layer], p["b_rt"][layer], n_groups, per_group)

    key = route[:, 2].astype(jnp.int32)
    rank = route[:, 3].astype(jnp.int32)
    counts = cnt[0, :n_buckets].astype(jnp.int32)
    padded = (counts + tb - 1) // tb * tb
    pend = jnp.cumsum(padded)
    pstart = pend - padded
    dest = pstart[key] + rank
    n_blocks = -(-(t + n_buckets * (tb - 1)) // tb)
    n_slots = n_blocks * tb
    blk_start = jnp.arange(n_blocks, dtype=jnp.int32) * tb
    blk_bucket = jnp.minimum(jnp.sum((pend[None, :] <= blk_start[:, None]).astype(jnp.int32), axis=1), n_buckets - 1)
    blk_nv = jnp.clip(counts[blk_bucket] - (blk_start - pstart[blk_bucket]), 0, tb).astype(jnp.int32)
    tab_a, tab_b = _pair_tables(n_groups, per_group)
    blk_ea, blk_eb = tab_a[blk_bucket], tab_b[blk_bucket]

    slot_tok = jnp.zeros((n_slots,), jnp.int32).at[dest].set(jnp.arange(t, dtype=jnp.int32))
    slot = jnp.arange(n_slots, dtype=jnp.int32)
    in_blk, blk = slot % tb, slot // tb
    valid = in_blk < blk_nv[blk]
    slot_dst = jnp.where(valid, slot_tok, t + (blk % 2) * tb + in_blk)
    slot_gate = jnp.where(valid[:, None], route[:, 0:2][slot_tok], 0.0)

    return _moe_mlp(x, t, p["g_ffn"][layer], p["w_gate"], p["w_up"], p["w_down"], layer, blk_ea, blk_eb, blk_nv,
                    slot_tok, slot_dst, slot_gate, tb, g_final)


def _rms_proj_kernel(x_ref, g_ref, w_ref, *refs):
    out_refs, hn_sc = refs[:-1], refs[-1]

    @pl.when(pl.program_id(1) == 0)
    def _():
        hn_sc[...] = _rms(x_ref[...], g_ref[...]).astype(BF16)

    res = jnp.dot(hn_sc[...], w_ref[...], preferred_element_type=F32)
    for o in out_refs:
        o[...] = res.astype(o.dtype)


def _rms_proj(x, t, g, w_bf, out_dtypes):
    d = x.shape[1]
    n = w_bf.shape[1]
    tm = _pick(t, (1024, 512, 256, 128))
    tn = _pick(n, (512, 256, 128))
    return pl.pallas_call(
        _rms_proj_kernel,
        grid=(t // tm, n // tn),
        in_specs=[
            pl.BlockSpec((tm, d), lambda i, j: (i, 0)),
            pl.BlockSpec((1, d), lambda i, j: (0, 0)),
            pl.BlockSpec((d, tn), lambda i, j: (0, j)),
        ],
        out_specs=[pl.BlockSpec((tm, tn), lambda i, j: (i, j)) for _ in out_dtypes],
        out_shape=[jax.ShapeDtypeStruct((t, n), dt) for dt in out_dtypes],
        scratch_shapes=[pltpu.VMEM((tm, d), BF16)],
        compiler_params=_params(("arbitrary", "arbitrary"), 48),
        name="rms_proj",
    )(x, g.reshape(1, d), w_bf)


def _proj_res_kernel(o_ref, w_ref, x_ref, out_ref):
    out_ref[...] = x_ref[...] + jnp.dot(o_ref[...], w_ref[...], preferred_element_type=F32)


def _proj_res(o_bf, w_bf, x, t):
    d = x.shape[1]
    k = o_bf.shape[1]
    tm = _pick(t, (1024, 512, 256, 128))
    tn = _pick(d, (512, 256, 128))
    return pl.pallas_call(
        _proj_res_kernel,
        grid=(t // tm, d // tn),
        in_specs=[
            pl.BlockSpec((tm, k), lambda i, j: (i, 0)),
            pl.BlockSpec((k, tn), lambda i, j: (0, j)),
            pl.BlockSpec((tm, tn), lambda i, j: (i, j)),
        ],
        out_specs=pl.BlockSpec((tm, tn), lambda i, j: (i, j)),
        out_shape=jax.ShapeDtypeStruct((t, d), F32),
        compiler_params=_params(("parallel", "parallel"), 48),
        name="proj_res",
    )(o_bf, w_bf, x)


def _attn_kernel(*refs, n_seg, n_heads, hd, scale, clamp_mask):
    q_ref = refs[0]
    kv_refs = refs[1:1 + 2 * n_seg]
    bias_ref = refs[1 + 2 * n_seg]
    o_ref = refs[-1]
    i = pl.program_id(1)

    def seg(ref, sl):
        v = ref[0, :, sl] if len(ref.shape) == 3 else ref[:, sl]
        return v.astype(BF16)

    for h in range(n_heads):
        sl = slice(h * hd, (h + 1) * hd)
        qh = q_ref[:, sl]
        scores = []
        off = 0
        for si in range(n_seg):
            kh = seg(kv_refs[2 * si], sl)
            nk = kh.shape[0]
            s = lax.dot_general(qh, kh, (((1,), (1,)), ((), ())), preferred_element_type=F32) * scale
            s = s + bias_ref[h, :, off:off + nk]
            if clamp_mask and si < n_seg - 1:
                s = jnp.where(i >= n_seg - 1 - si, s, NEG_INF)
            scores.append(s)
            off += nk
        m = functools.reduce(jnp.maximum, [jnp.max(s, axis=-1, keepdims=True) for s in scores])
        ps = [jnp.exp(s - m) for s in scores]
        l = functools.reduce(lambda a, b: a + b, [jnp.sum(p, axis=-1, keepdims=True) for p in ps])
        acc = None
        for si in range(n_seg):
            pv = jnp.dot(ps[si].astype(BF16), seg(kv_refs[2 * si + 1], sl), preferred_element_type=F32)
            acc = pv if acc is None else acc + pv
        o_ref[:, sl] = (acc / l).astype(o_ref.dtype)


def _band_bias(table, qb, n_keys, left):
    clip = (table.shape[-1] - 1) // 2
    n_heads = table.shape[0]
    w = qb + n_keys
    m = jnp.arange(w, dtype=jnp.int32)
    delta = jnp.where(m < n_keys, m, m - w)
    u = table[:, jnp.clip(left - delta, -clip, clip) + clip].astype(F32)
    toe = jnp.tile(u, (1, qb))[:, :qb * (w - 1)].reshape(n_heads, qb, w - 1)[:, :, :n_keys]
    i = jnp.arange(qb, dtype=jnp.int32)[:, None]
    j = jnp.arange(n_keys, dtype=jnp.int32)[None, :]
    kpos = j - left
    cq = (i // CHUNK) * CHUNK
    inband = (kpos >= cq - LEFT_LEN) & (kpos < cq + CHUNK)
    return jnp.where(inband[None], toe, NEG_INF)


def _attn_prompt(q, kv, table, batch, seq, n_heads, n_tok_total):
    d = q.shape[1]
    hd = d // n_heads
    qb = _pick(seq, (256, 128, 64))
    nq = seq // qb
    n_seg = LEFT_LEN // qb + 1
    bias = _band_bias(table, qb, n_seg * qb, LEFT_LEN)

    def kv_spec(si, col):
        back = n_seg - 1 - si
        return pl.BlockSpec((qb, d), lambda b, i: (b * nq + jnp.maximum(i - back, 0), col))

    in_specs = [pl.BlockSpec((qb, d), lambda b, i: (b * nq + i, 0))]
    args = [q]
    for si in range(n_seg):
        in_specs += [kv_spec(si, 0), kv_spec(si, 1)]
        args += [kv, kv]
    in_specs.append(pl.BlockSpec((n_heads, qb, n_seg * qb), lambda b, i: (0, 0, 0)))
    args.append(bias)
    return pl.pallas_call(
        functools.partial(_attn_kernel, n_seg=n_seg, n_heads=n_heads, hd=hd, scale=hd ** -0.5,
                          clamp_mask=True),
        grid=(batch, nq),
        in_specs=in_specs,
        out_specs=pl.BlockSpec((qb, d), lambda b, i: (b * nq + i, 0)),
        out_shape=jax.ShapeDtypeStruct((n_tok_total, d), BF16),
        compiler_params=_params(("parallel", "arbitrary"), 56),
        name="attn_prompt",
    )(*args)


def _attn_sample_kernel(*refs, **kw):
    _attn_kernel(*refs[:-2], refs[-1], **kw)


def _attn_sample(q, kv, cache_k, cache_v, table, row0, batch, seq, n_heads, o_buf):
    d = q.shape[1]
    hd = d // n_heads
    past = cache_k.shape[1]
    assert row0 % seq == 0
    rb = row0 // seq
    assert seq <= CHUNK and past == LEFT_LEN
    bias = _band_bias(table, seq, past + seq, past)
    in_specs = [
        pl.BlockSpec((seq, d), lambda b, i: (rb + b, 0)),
        pl.BlockSpec((1, past, d), lambda b, i: (b, 0, 0)),
        pl.BlockSpec((1, past, d), lambda b, i: (b, 0, 0)),
        pl.BlockSpec((seq, d), lambda b, i: (rb + b, 0)),
        pl.BlockSpec((seq, d), lambda b, i: (rb + b, 1)),
        pl.BlockSpec((n_heads, seq, past + seq), lambda b, i: (0, 0, 0)),
        pl.BlockSpec(memory_space=pl.ANY),
    ]
    return pl.pallas_call(
        functools.partial(_attn_sample_kernel, n_seg=2, n_heads=n_heads, hd=hd, scale=hd ** -0.5,
                          clamp_mask=False),
        grid=(batch, 1),
        in_specs=in_specs,
        out_specs=pl.BlockSpec((seq, d), lambda b, i: (rb + b, 0)),
        out_shape=jax.ShapeDtypeStruct(o_buf.shape, BF16),
        input_output_aliases={6: 0},
        compiler_params=_params(("parallel", "arbitrary"), 56),
        name="attn_sample",
    )(q, cache_k, cache_v, kv, kv, bias, o_buf)


def kernel(x_prompt, x_sample, state_pool, cache_k, cache_v, g_pool, w_pool, s_pool, g_kv, w_kv, g_attn, w_q, w_o, rel_bias, g_ffn, w_group, b_group, w_router, b_router, w_gate, w_up, w_down, g_final):
    bp, lp, d = x_prompt.shape
    bs, ls, _ = x_sample.shape
    depth = g_ffn.shape[0]
    n_pool = g_pool.shape[0]
    n_heads, hd = cache_k.shape[2], cache_k.shape[3]
    n_groups, n_experts = w_group.shape[-1], w_router.shape[-1]
    tp, tsm = bp * lp, bs * ls
    t = tp + tsm
    tb = 128

    w_rt = jnp.concatenate([w_group, w_router], axis=-1)
    w_rt = jnp.pad(w_rt, ((0, 0), (0, 0), (0, LANES - w_rt.shape[-1]))).astype(BF16)
    b_rt = jnp.concatenate([b_group, b_router], axis=-1)
    b_rt = jnp.pad(b_rt, ((0, 0), (0, LANES - b_rt.shape[-1])))[:, None, :]
    moe_p = dict(n_groups=n_groups, n_experts=n_experts, g_ffn=g_ffn, w_rt=w_rt, b_rt=b_rt,
                 w_gate=w_gate.astype(BF16), w_up=w_up.astype(BF16), w_down=w_down.astype(BF16))
    w_pool_bf = w_pool.astype(BF16)
    w_kv_bf = w_kv.astype(BF16)
    w_q_bf = w_q.astype(BF16)
    w_o_bf = w_o.astype(BF16)

    zero_halo = jnp.zeros((1, HALO, d), F32)
    samp_halo = jnp.pad(state_pool, ((0, 0), (0, 0), (1, 0), (0, 0)))
    ck = cache_k.reshape(bs, cache_k.shape[1], d)
    cv = cache_v.reshape(bs, cache_v.shape[1], d)

    x = None
    hist_p, hist_s = [], []
    kv_f32 = kv_bf = None
    for layer in range(depth):
        if layer < n_pool:
            if layer == 0:
                src_p, off_p, src_s, off_s = x_prompt.reshape(tp, d), 0, x_sample.reshape(tsm, d), 0
            else:
                src_p, off_p, src_s, off_s = x, 0, x, tp
            buf, hp = _pool_layer(src_p, off_p, None, 0, t, bp, lp, zero_halo, g_pool[layer], w_pool_bf[layer],
                                  s_pool[layer], 0)
            x, hs = _pool_layer(src_s, off_s, buf, tp, t, bs, ls, samp_halo[layer], g_pool[layer], w_pool_bf[layer],
                                s_pool[layer], PAST_LEN)
            hist_p.append(hp)
            hist_s.append(hs)
        else:
            j = layer - n_pool
            if j == 0:
                kv_f32, kv_bf = _rms_proj(x, t, g_kv, w_kv_bf, (F32, BF16))
            (q,) = _rms_proj(x, t, g_attn[j], w_q_bf[j], (BF16,))
            o = _attn_prompt(q, kv_bf, rel_bias[j], bp, lp, n_heads, t)
            o = _attn_sample(q, kv_bf, ck, cv, rel_bias[j], tp, bs, ls, n_heads, o)
            x = _proj_res(o, w_o_bf[j], x, t)
        x = _hier_moe(x, t, layer, moe_p, tb, g_final if layer == depth - 1 else None)

    y_prompt = x[:tp].reshape(bp, lp, d)
    y_sample = x[tp:t].reshape(bs, ls, d)
    keep = min(LEFT_LEN, lp)
    kvp = kv_f32[:tp].reshape(bp, lp, 2 * d)[:, lp - keep:]
    kvs = kv_f32[tp:].reshape(bs, ls, 2 * d)
    return (y_prompt, y_sample, jnp.stack(hist_p, axis=0), jnp.stack(hist_s, axis=0),
            kvp[..., :d].reshape(bp, keep, n_heads, hd), kvp[..., d:].reshape(bp, keep, n_heads, hd),
            kvs[..., :d].reshape(bs, ls, n_heads, hd), kvs[..., d:].reshape(bs, ls, n_heads, hd))
```

```python
import functools

import jax
import jax.numpy as jnp
from jax import lax
from jax.experimental import pallas as pl
from jax.experimental.pallas import tpu as pltpu

RMS_EPS = 1e-6
NEG_INF = -1e30
CHUNK = 64
LEFT_CHUNKS = 8
LEFT_LEN = LEFT_CHUNKS * CHUNK
POOL_WINDOWS = (2, 4, 8, 16)
POOL_HIST = max(POOL_WINDOWS) - 1
HALO = POOL_HIST + 1
PAST_LEN = 1024
LANES = 128

F32 = jnp.float32
BF16 = jnp.bfloat16


def _pick(n, cands):
    for c in cands:
        if n % c == 0:
            return c
    raise ValueError(f"no tile for {n} in {cands}")


def _params(sem, vmem_mib=None):
    return pltpu.CompilerParams(
        dimension_semantics=sem,
        vmem_limit_bytes=None if vmem_mib is None else vmem_mib * 1024 * 1024)


def _rms(x, g):
    ms = jnp.mean(x * x, axis=-1, keepdims=True)
    return (x * lax.rsqrt(ms + RMS_EPS)) * g


def _pool_kernel(*refs, ts, dg, pos0, aliased):
    if aliased:
        x_ref, halo0_ref, g_ref, w_ref, s_ref, _, out_ref, hist_ref, halo_sc = refs
    else:
        x_ref, halo0_ref, g_ref, w_ref, s_ref, out_ref, hist_ref, halo_sc = refs
    s = pl.program_id(1)
    ns = pl.num_programs(1)

    @pl.when(s == 0)
    def _():
        halo_sc[...] = halo0_ref[0]

    x = x_ref[...]
    h = _rms(x, g_ref[...])
    halo = halo_sc[...]
    pos = lax.broadcasted_iota(jnp.int32, (ts, 1), 0) + (s * ts + pos0)
    for gi, w in enumerate(POOL_WINDOWS):
        lo, hi = gi * dg, (gi + 1) * dg
        hg = h[:, lo:hi]
        acc = jnp.concatenate([halo[:, lo:hi], hg], axis=0)
        k = 1
        while k < w:
            acc = acc + pltpu.roll(acc, k, 0)
            k *= 2
        cnt = jnp.minimum(pos + 1, w).astype(F32)
        pooled = acc[HALO:, :] * (1.0 / cnt) - hg
        mix = jnp.dot(pooled.astype(BF16), w_ref[gi], preferred_element_type=F32) * s_ref[:, lo:hi]
        out_ref[:, lo:hi] = x[:, lo:hi] + mix
    halo_sc[...] = h[ts - HALO:, :]

    @pl.when(s == ns - 1)
    def _():
        hist_ref[0] = halo_sc[1:HALO, :]


def _pool_layer(x_in, in_off, out_buf, out_off, n_tok_total, batch, seq, halo0, g, w_bf, s, pos0):
    d = x_in.shape[1]
    dg = d // len(POOL_WINDOWS)
    ts = _pick(seq, (256, 128, 64))
    ns = seq // ts
    assert in_off % ts == 0 and out_off % ts == 0
    ib, ob = in_off // ts, out_off // ts
    aliased = out_buf is not None
    halo_b = halo0.shape[0]
    in_specs = [
        pl.BlockSpec((ts, d), lambda b, i: (ib + b * ns + i, 0)),
        pl.BlockSpec((1, HALO, d), (lambda b, i: (b, 0, 0)) if halo_b > 1 else (lambda b, i: (0, 0, 0))),
        pl.BlockSpec((1, d), lambda b, i: (0, 0)),
        pl.BlockSpec((len(POOL_WINDOWS), dg, dg), lambda b, i: (0, 0, 0)),
        pl.BlockSpec((1, d), lambda b, i: (0, 0)),
    ]
    args = [x_in, halo0, g.reshape(1, d), w_bf, s.reshape(1, d)]
    if aliased:
        in_specs.append(pl.BlockSpec(memory_space=pl.ANY))
        args.append(out_buf)
    return pl.pallas_call(
        functools.partial(_pool_kernel, ts=ts, dg=dg, pos0=pos0, aliased=aliased),
        grid=(batch, ns),
        in_specs=in_specs,
        out_specs=[
            pl.BlockSpec((ts, d), lambda b, i: (ob + b * ns + i, 0)),
            pl.BlockSpec((1, POOL_HIST, d), lambda b, i: (b, 0, 0)),
        ],
        out_shape=[
            jax.ShapeDtypeStruct((n_tok_total, d), F32),
            jax.ShapeDtypeStruct((batch, POOL_HIST, d), F32),
        ],
        scratch_shapes=[pltpu.VMEM((HALO, d), F32)],
        input_output_aliases={len(args) - 1: 0} if aliased else {},
        compiler_params=_params(("arbitrary", "arbitrary"), 48),
        name="pool_layer",
    )(*args)


def _router_kernel(x_ref, g_ref, w_ref, b_ref, route_ref, cnt_ref, carry_sc, *, tm, n_groups, per_group):
    i = pl.program_id(0)

    @pl.when(i == 0)
    def _():
        carry_sc[...] = jnp.zeros_like(carry_sc)

    h = _rms(x_ref[...], g_ref[...])
    lg = jnp.dot(h.astype(BF16), w_ref[...], preferred_element_type=F32) + b_ref[...]
    lane = lax.broadcasted_iota(jnp.int32, (tm, LANES), 1).astype(F32)
    big = float(LANES)

    def first_max(vals, mask):
        v = jnp.max(jnp.where(mask, vals, -jnp.inf), axis=-1, keepdims=True)
        idx = jnp.min(jnp.where(mask & (vals == v), lane, big), axis=-1, keepdims=True)
        return v, idx

    gmask = lane < n_groups
    _, gsel = first_max(lg, gmask)
    lo = n_groups + gsel * per_group
    emask = (lane >= lo) & (lane < lo + per_group)
    _, i1 = first_max(lg, emask)
    _, i2 = first_max(lg, emask & (lane != i1))

    a = jnp.minimum(i1, i2) - lo
    b = jnp.maximum(i1, i2) - lo
    n_pairs = per_group * (per_group - 1) // 2
    key = gsel * n_pairs + a * (2 * per_group - 1 - a) * 0.5 + (b - a - 1.0)
    sel = lane == key
    onehot = jnp.where(sel, 1.0, 0.0)
    r_i = lax.broadcasted_iota(jnp.int32, (tm, tm), 0)
    c_i = lax.broadcasted_iota(jnp.int32, (tm, tm), 1)
    lower = jnp.where(c_i < r_i, 1.0, 0.0).astype(BF16)
    before = jnp.dot(lower, onehot.astype(BF16), preferred_element_type=F32) + carry_sc[...]
    rank = jnp.sum(jnp.where(sel, before, 0.0), axis=-1, keepdims=True)
    carry = carry_sc[...] + jnp.sum(onehot, axis=0, keepdims=True)
    carry_sc[...] = carry
    cnt_ref[...] = carry

    route_ref[...] = jnp.where(lane == 0, key, jnp.where(lane == 1, rank, 0.0))


def _route(x, t, g, w_rt, b_rt, n_groups, per_group):
    d = x.shape[1]
    tm = _pick(t, (512, 256, 128))
    return pl.pallas_call(
        functools.partial(_router_kernel, tm=tm, n_groups=n_groups, per_group=per_group),
        grid=(t // tm,),
        in_specs=[
            pl.BlockSpec((tm, d), lambda i: (i, 0)),
            pl.BlockSpec((1, d), lambda i: (0, 0)),
            pl.BlockSpec((d, LANES), lambda i: (0, 0)),
            pl.BlockSpec((1, LANES), lambda i: (0, 0)),
        ],
        out_specs=[
            pl.BlockSpec((tm, LANES), lambda i: (i, 0)),
            pl.BlockSpec((1, LANES), lambda i: (0, 0)),
        ],
        out_shape=[
            jax.ShapeDtypeStruct((t, LANES), F32),
            jax.ShapeDtypeStruct((1, LANES), F32),
        ],
        scratch_shapes=[pltpu.VMEM((1, LANES), F32)],
        compiler_params=_params(("arbitrary",), 48),
        name="moe_route",
    )(x, g.reshape(1, d), w_rt, b_rt)


def _moe_kernel(ea_ref, eb_ref, nv_ref, srcc_ref, srcn_ref, dst_ref, x_hbm, g_ref, wrt_ref, brt_ref,
                wga_ref, wua_ref, wda_ref, wgb_ref, wub_ref, wdb_ref, *rest, tb, n_tok, n_groups, final):
    if final:
        gf_ref, out_hbm, xbuf, obuf, gsem, ssem = rest
    else:
        out_hbm, xbuf, obuf, gsem, ssem = rest
    b = pl.program_id(0)
    nb = pl.num_programs(0)
    slot = lax.rem(b, 2)
    nv = nv_ref[b]
    nv_next = nv_ref[jnp.minimum(b + 1, nb - 1)]
    nv_prev2 = nv_ref[jnp.maximum(b - 2, 0)]
    nv_prev1 = nv_ref[jnp.maximum(b - 1, 0)]

    def gather(idx_ref, s):
        for r in range(tb):
            pltpu.make_async_copy(x_hbm.at[pl.ds(idx_ref[0, 0, r], 1), :], xbuf.at[s, pl.ds(r, 1), :],
                                  gsem.at[s]).start(priority=r % 2)

    def gather_wait(s):
        pltpu.make_async_copy(x_hbm.at[pl.ds(0, tb), :], xbuf.at[s], gsem.at[s]).wait()

    def scatter_wait(s):
        pltpu.make_async_copy(obuf.at[s], out_hbm.at[pl.ds(0, tb), :], ssem.at[s]).wait()

    @pl.when(b == 0)
    def _():
        obuf[...] = jnp.zeros_like(obuf)
        for s in range(2):
            spare = pltpu.make_async_copy(obuf.at[s], out_hbm.at[pl.ds(n_tok + s * tb, tb), :], ssem.at[s])
            spare.start()
            spare.wait()

    @pl.when((b == 0) & (nv > 0))
    def _():
        gather(srcc_ref, 0)

    @pl.when((b + 1 < nb) & (nv_next > 0))
    def _():
        gather(srcn_ref, 1 - slot)

    @pl.when((b >= 2) & (nv_prev2 > 0))
    def _():
        scatter_wait(slot)

    @pl.when(nv > 0)
    def _():
        gather_wait(slot)
        x = xbuf[slot]
        hn = _rms(x, g_ref[...]).astype(BF16)

        def expert(wg_ref, wu_ref, wd_ref):
            a = jnp.dot(hn, wg_ref[...], preferred_element_type=F32)
            u = jnp.dot(hn, wu_ref[...], preferred_element_type=F32)
            act = (a * (1.0 / (1.0 + jnp.exp(-a)))) * u
            return jnp.dot(act.astype(BF16), wd_ref[...], preferred_element_type=F32)

        lg = jnp.dot(hn, wrt_ref[...], preferred_element_type=F32) + brt_ref[...]
        lane = lax.broadcasted_iota(jnp.int32, (tb, LANES), 1)
        gmask = lane < n_groups
        gmax = jnp.max(jnp.where(gmask, lg, -jnp.inf), axis=-1, keepdims=True)
        g_prob = 1.0 / jnp.sum(jnp.where(gmask, jnp.exp(lg - gmax), 0.0), axis=-1, keepdims=True)
        la = jnp.sum(jnp.where(lane == n_groups + ea_ref[b], lg, 0.0), axis=-1, keepdims=True)
        lb = jnp.sum(jnp.where(lane == n_groups + eb_ref[b], lg, 0.0), axis=-1, keepdims=True)
        top = jnp.maximum(la, lb)
        pa = jnp.exp(la - top)
        pb = jnp.exp(lb - top)
        den = pa + pb
        gate_a = (pa / den) * g_prob
        gate_b = (pb / den) * g_prob

        ya = expert(wga_ref, wua_ref, wda_ref)
        yb = expert(wgb_ref, wub_ref, wdb_ref)
        out = x + (gate_a * ya + gate_b * yb)
        if final:
            out = _rms(out, gf_ref[...])
        obuf[slot] = out

        for r in range(tb):
            pltpu.make_async_copy(obuf.at[slot, pl.ds(r, 1), :], out_hbm.at[pl.ds(dst_ref[0, 0, r], 1), :],
                                  ssem.at[slot]).start(priority=r % 2)

    @pl.when(b == nb - 1)
    def _():
        @pl.when((b >= 1) & (nv_prev1 > 0))
        def _():
            scatter_wait(1 - slot)

        @pl.when(nv > 0)
        def _():
            scatter_wait(slot)


def _moe_mlp(x, t, g, w_rt, b_rt, n_groups, wg, wu, wd, layer, blk_ea, blk_eb, blk_nv, slot_src, slot_dst, tb,
             g_final):
    d = x.shape[1]
    de = wg.shape[-1]
    nb = blk_nv.shape[0]
    src3 = slot_src.reshape(nb, 1, tb)
    dst3 = slot_dst.reshape(nb, 1, tb)
    final = g_final is not None

    def wspec(which, shape):
        return pl.BlockSpec((None, None) + shape, lambda b, ea, eb, nv: (layer, (ea, eb)[which][b], 0, 0))

    in_specs = [
        pl.BlockSpec((1, 1, tb), lambda b, ea, eb, nv: (b, 0, 0), memory_space=pltpu.SMEM),
        pl.BlockSpec((1, 1, tb), lambda b, ea, eb, nv: (jnp.minimum(b + 1, nb - 1), 0, 0), memory_space=pltpu.SMEM),
        pl.BlockSpec((1, 1, tb), lambda b, ea, eb, nv: (b, 0, 0), memory_space=pltpu.SMEM),
        pl.BlockSpec(memory_space=pl.ANY),
        pl.BlockSpec((1, d), lambda b, ea, eb, nv: (0, 0)),
        pl.BlockSpec((d, LANES), lambda b, ea, eb, nv: (0, 0)),
        pl.BlockSpec((1, LANES), lambda b, ea, eb, nv: (0, 0)),
        wspec(0, (d, de)), wspec(0, (d, de)), wspec(0, (de, d)),
        wspec(1, (d, de)), wspec(1, (d, de)), wspec(1, (de, d)),
    ]
    args = [blk_ea, blk_eb, blk_nv, src3, src3, dst3, x, g.reshape(1, d), w_rt, b_rt, wg, wu, wd, wg, wu, wd]
    if final:
        in_specs.append(pl.BlockSpec((1, d), lambda b, ea, eb, nv: (0, 0)))
        args.append(g_final.reshape(1, d))
    grid_spec = pltpu.PrefetchScalarGridSpec(
        num_scalar_prefetch=3,
        grid=(nb,),
        in_specs=in_specs,
        out_specs=pl.BlockSpec(memory_space=pl.ANY),
        scratch_shapes=[pltpu.VMEM((2, tb, d), F32), pltpu.VMEM((2, tb, d), F32),
                        pltpu.SemaphoreType.DMA((2,)), pltpu.SemaphoreType.DMA((2,))],
    )
    return pl.pallas_call(
        functools.partial(_moe_kernel, tb=tb, n_tok=t, n_groups=n_groups, final=final),
        grid_spec=grid_spec,
        out_shape=jax.ShapeDtypeStruct((t + 2 * tb, d), F32),
        compiler_params=_params(("arbitrary",), 56),
        name="moe_mlp",
    )(*args)


def _pair_tables(n_groups, per_group):
    ea, eb = [], []
    for grp in range(n_groups):
        for a in range(per_group):
            for b in range(a + 1, per_group):
                ea.append(grp * per_group + a)
                eb.append(grp * per_group + b)
    return jnp.asarray(ea, jnp.int32), jnp.asarray(eb, jnp.int32)


def _hier_moe(x, t, layer, p, tb, g_final=None):
    n_groups, n_experts = p["n_groups"], p["n_experts"]
    per_group = n_experts // n_groups
    n_buckets = n_groups * (per_group * (per_group - 1) // 2)
    assert n_buckets <= LANES
    route, cnt = _route(x, t, p["g_ffn"][layer], p["w_rt"][layer], p["b_rt"][layer], n_groups, per_group)

    key = route[:, 0].astype(jnp.int32)
    rank = route[:, 1].astype(jnp.int32)
    counts = cnt[0, :n_buckets].astype(jnp.int32)
    padded = (counts + tb - 1) // tb * tb
    pend = jnp.cumsum(padded)
    pstart = pend - padded
    bucket_ids = jnp.arange(n_buckets, dtype=jnp.int32)
    dest = jnp.sum(jnp.where(key[:, None] == bucket_ids[None, :], pstart[None, :], 0), axis=1) + rank
    n_blocks = -(-(t + n_buckets * (tb - 1)) // tb)
    n_slots = n_blocks * tb
    blk_start = jnp.arange(n_blocks, dtype=jnp.int32) * tb
    blk_bucket = jnp.minimum(jnp.sum((pend[None, :] <= blk_start[:, None]).astype(jnp.int32), axis=1), n_buckets - 1)
    blk_nv = jnp.clip(counts[blk_bucket] - (blk_start - pstart[blk_bucket]), 0, tb).astype(jnp.int32)
    tab_a, tab_b = _pair_tables(n_groups, per_group)
    blk_ea, blk_eb = tab_a[blk_bucket], tab_b[blk_bucket]

    slot_tok = jnp.zeros((n_slots,), jnp.int32).at[dest].set(jnp.arange(t, dtype=jnp.int32))
    slot = jnp.arange(n_slots, dtype=jnp.int32)
    in_blk, blk = slot % tb, slot // tb
    valid = in_blk < jnp.repeat(blk_nv, tb)
    slot_dst = jnp.where(valid, slot_tok, t + (blk % 2) * tb + in_blk)

    return _moe_mlp(x, t, p["g_ffn"][layer], p["w_rt"][layer], p["b_rt"][layer], n_groups, p["w_gate"], p["w_up"],
                    p["w_down"], layer, blk_ea, blk_eb, blk_nv, slot_tok, slot_dst, tb, g_final)


def _rms_proj_kernel(x_ref, g_ref, w_ref, *refs):
    out_refs, hn_sc = refs[:-1], refs[-1]

    @pl.when(pl.program_id(1) == 0)
    def _():
        hn_sc[...] = _rms(x_ref[...], g_ref[...]).astype(BF16)

    res = jnp.dot(hn_sc[...], w_ref[...], preferred_element_type=F32)
    for o in out_refs:
        o[...] = res.astype(o.dtype)


def _rms_proj(x, t, g, w_bf, out_dtypes):
    d = x.shape[1]
    n = w_bf.shape[1]
    tm = _pick(t, (1024, 512, 256, 128))
    tn = _pick(n, (512, 256, 128))
    return pl.pallas_call(
        _rms_proj_kernel,
        grid=(t // tm, n // tn),
        in_specs=[
            pl.BlockSpec((tm, d), lambda i, j: (i, 0)),
            pl.BlockSpec((1, d), lambda i, j: (0, 0)),
            pl.BlockSpec((d, tn), lambda i, j: (0, j)),
        ],
        out_specs=[pl.BlockSpec((tm, tn), lambda i, j: (i, j)) for _ in out_dtypes],
        out_shape=[jax.ShapeDtypeStruct((t, n), dt) for dt in out_dtypes],
        scratch_shapes=[pltpu.VMEM((tm, d), BF16)],
        compiler_params=_params(("arbitrary", "arbitrary"), 48),
        name="rms_proj",
    )(x, g.reshape(1, d), w_bf)


def _proj_res_kernel(o_ref, w_ref, x_ref, out_ref):
    out_ref[...] = x_ref[...] + jnp.dot(o_ref[...], w_ref[...], preferred_element_type=F32)


def _proj_res(o_bf, w_bf, x, t):
    d = x.shape[1]
    k = o_bf.shape[1]
    tm = _pick(t, (1024, 512, 256, 128))
    tn = _pick(d, (512, 256, 128))
    return pl.pallas_call(
        _proj_res_kernel,
        grid=(t // tm, d // tn),
        in_specs=[
            pl.BlockSpec((tm, k), lambda i, j: (i, 0)),
            pl.BlockSpec((k, tn), lambda i, j: (0, j)),
            pl.BlockSpec((tm, tn), lambda i, j: (i, j)),
        ],
        out_specs=pl.BlockSpec((tm, tn), lambda i, j: (i, j)),
        out_shape=jax.ShapeDtypeStruct((t, d), F32),
        compiler_params=_params(("parallel", "parallel"), 48),
        name="proj_res",
    )(o_bf, w_bf, x)


def _attn_kernel(*refs, n_seg, n_heads, hd, scale, clamp_mask):
    q_ref = refs[0]
    kv_refs = refs[1:1 + 2 * n_seg]
    bias_ref = refs[1 + 2 * n_seg]
    o_ref = refs[-1]
    i = pl.program_id(1)

    def seg(ref, h):
        sl = slice(h * hd, (h + 1) * hd)
        return ref[0, :, sl] if len(ref.shape) == 3 else ref[:, sl]

    for h in range(n_heads):
        sl = slice(h * hd, (h + 1) * hd)
        qh = q_ref[:, sl]
        scores = []
        off = 0
        for si in range(n_seg):
            kh = seg(kv_refs[2 * si], h)
            nk = kh.shape[0]
            s = lax.dot_general(qh, kh, (((1,), (1,)), ((), ())), preferred_element_type=F32) * scale
            s = s + bias_ref[h, :, off:off + nk]
            if clamp_mask and si < n_seg - 1:
                s = jnp.where(i >= n_seg - 1 - si, s, NEG_INF)
            scores.append(s)
            off += nk
        m = functools.reduce(jnp.maximum, [jnp.max(s, axis=-1, keepdims=True) for s in scores])
        ps = [jnp.exp(s - m) for s in scores]
        l = functools.reduce(lambda a, b: a + b, [jnp.sum(p, axis=-1, keepdims=True) for p in ps])
        acc = None
        for si in range(n_seg):
            pv = jnp.dot(ps[si].astype(BF16), seg(kv_refs[2 * si + 1], h), preferred_element_type=F32)
            acc = pv if acc is None else acc + pv
        o_ref[:, sl] = (acc / l).astype(o_ref.dtype)


def _band_bias(table, qb, n_keys, left):
    clip = (table.shape[-1] - 1) // 2
    n_heads = table.shape[0]
    w = qb + n_keys
    m = jnp.arange(w, dtype=jnp.int32)
    delta = jnp.where(m < n_keys, m, m - w)
    u = table[:, jnp.clip(left - delta, -clip, clip) + clip].astype(F32)
    toe = jnp.tile(u, (1, qb))[:, :qb * (w - 1)].reshape(n_heads, qb, w - 1)[:, :, :n_keys]
    i = jnp.arange(qb, dtype=jnp.int32)[:, None]
    j = jnp.arange(n_keys, dtype=jnp.int32)[None, :]
    kpos = j - left
    cq = (i // CHUNK) * CHUNK
    inband = (kpos >= cq - LEFT_LEN) & (kpos < cq + CHUNK)
    return jnp.where(inband[None], toe, NEG_INF)


def _attn_prompt(q, kv, table, batch, seq, n_heads, n_tok_total):
    d = q.shape[1]
    hd = d // n_heads
    qb = _pick(seq, (256, 128, 64))
    nq = seq // qb
    n_seg = LEFT_LEN // qb + 1
    bias = _band_bias(table, qb, n_seg * qb, LEFT_LEN)

    def kv_spec(si, col):
        back = n_seg - 1 - si
        return pl.BlockSpec((qb, d), lambda b, i: (b * nq + jnp.maximum(i - back, 0), col))

    in_specs = [pl.BlockSpec((qb, d), lambda b, i: (b * nq + i, 0))]
    args = [q]
    for si in range(n_seg):
        in_specs += [kv_spec(si, 0), kv_spec(si, 1)]
        args += [kv, kv]
    in_specs.append(pl.BlockSpec((n_heads, qb, n_seg * qb), lambda b, i: (0, 0, 0)))
    args.append(bias)
    return pl.pallas_call(
        functools.partial(_attn_kernel, n_seg=n_seg, n_heads=n_heads, hd=hd, scale=hd ** -0.5,
                          clamp_mask=True),
        grid=(batch, nq),
        in_specs=in_specs,
        out_specs=pl.BlockSpec((qb, d), lambda b, i: (b * nq + i, 0)),
        out_shape=jax.ShapeDtypeStruct((n_tok_total, d), BF16),
        compiler_params=_params(("parallel", "arbitrary"), 56),
        name="attn_prompt",
    )(*args)


def _attn_sample_kernel(*refs, **kw):
    _attn_kernel(*refs[:-2], refs[-1], **kw)


def _attn_sample(q, kv, cache_k, cache_v, table, row0, batch, seq, n_heads, o_buf):
    d = q.shape[1]
    hd = d // n_heads
    past = cache_k.shape[1]
    assert row0 % seq == 0
    rb = row0 // seq
    assert seq <= CHUNK and past == LEFT_LEN
    bias = _band_bias(table, seq, past + seq, past)
    in_specs = [
        pl.BlockSpec((seq, d), lambda b, i: (rb + b, 0)),
        pl.BlockSpec((1, past, d), lambda b, i: (b, 0, 0)),
        pl.BlockSpec((1, past, d), lambda b, i: (b, 0, 0)),
        pl.BlockSpec((seq, d), lambda b, i: (rb + b, 0)),
        pl.BlockSpec((seq, d), lambda b, i: (rb + b, 1)),
        pl.BlockSpec((n_heads, seq, past + seq), lambda b, i: (0, 0, 0)),
        pl.BlockSpec(memory_space=pl.ANY),
    ]
    return pl.pallas_call(
        functools.partial(_attn_sample_kernel, n_seg=2, n_heads=n_heads, hd=hd, scale=hd ** -0.5,
                          clamp_mask=False),
        grid=(batch, 1),
        in_specs=in_specs,
        out_specs=pl.BlockSpec((seq, d), lambda b, i: (rb + b, 0)),
        out_shape=jax.ShapeDtypeStruct(o_buf.shape, BF16),
        input_output_aliases={6: 0},
        compiler_params=_params(("parallel", "arbitrary"), 56),
        name="attn_sample",
    )(q, cache_k, cache_v, kv, kv, bias, o_buf)


def kernel(x_prompt, x_sample, state_pool, cache_k, cache_v, g_pool, w_pool, s_pool, g_kv, w_kv, g_attn, w_q, w_o, rel_bias, g_ffn, w_group, b_group, w_router, b_router, w_gate, w_up, w_down, g_final):
    bp, lp, d = x_prompt.shape
    bs, ls, _ = x_sample.shape
    depth = g_ffn.shape[0]
    n_pool = g_pool.shape[0]
    n_heads, hd = cache_k.shape[2], cache_k.shape[3]
    n_groups, n_experts = w_group.shape[-1], w_router.shape[-1]
    tp, tsm = bp * lp, bs * ls
    t = tp + tsm
    tb = 128

    w_rt = jnp.concatenate([w_group, w_router], axis=-1)
    w_rt = jnp.pad(w_rt, ((0, 0), (0, 0), (0, LANES - w_rt.shape[-1]))).astype(BF16)
    b_rt = jnp.concatenate([b_group, b_router], axis=-1)
    b_rt = jnp.pad(b_rt, ((0, 0), (0, LANES - b_rt.shape[-1])))[:, None, :]
    moe_p = dict(n_groups=n_groups, n_experts=n_experts, g_ffn=g_ffn, w_rt=w_rt, b_rt=b_rt,
                 w_gate=w_gate.astype(BF16), w_up=w_up.astype(BF16), w_down=w_down.astype(BF16))
    w_pool_bf = w_pool.astype(BF16)
    w_kv_bf = w_kv.astype(BF16)
    w_q_bf = w_q.astype(BF16)
    w_o_bf = w_o.astype(BF16)

    zero_halo = jnp.zeros((1, HALO, d), F32)
    samp_halo = jnp.pad(state_pool, ((0, 0), (0, 0), (1, 0), (0, 0)))

    ck = cache_k.reshape(bs, cache_k.shape[1], d).astype(BF16)
    cv = cache_v.reshape(bs, cache_v.shape[1], d).astype(BF16)

    x = None
    hist_p, hist_s = [], []
    kv_f32 = kv_bf = None
    for layer in range(depth):
        if layer < n_pool:
            if layer == 0:
                src_p, off_p, src_s, off_s = x_prompt.reshape(tp, d), 0, x_sample.reshape(tsm, d), 0
            else:
                src_p, off_p, src_s, off_s = x, 0, x, tp
            buf, hp = _pool_layer(src_p, off_p, None, 0, t, bp, lp, zero_halo, g_pool[layer], w_pool_bf[layer],
                                  s_pool[layer], 0)
            x, hs = _pool_layer(src_s, off_s, buf, tp, t, bs, ls, samp_halo[layer], g_pool[layer], w_pool_bf[layer],
                                s_pool[layer], PAST_LEN)
            hist_p.append(hp)
            hist_s.append(hs)
        else:
            j = layer - n_pool
            if j == 0:
                kv_f32, kv_bf = _rms_proj(x, t, g_kv, w_kv_bf, (F32, BF16))
            (q,) = _rms_proj(x, t, g_attn[j], w_q_bf[j], (BF16,))
            o = _attn_prompt(q, kv_bf, rel_bias[j], bp, lp, n_heads, t)
            o = _attn_sample(q, kv_bf, ck, cv, rel_bias[j], tp, bs, ls, n_heads, o)
            x = _proj_res(o, w_o_bf[j], x, t)
        x = _hier_moe(x, t, layer, moe_p, tb, g_final if layer == depth - 1 else None)

    y_prompt = x[:tp].reshape(bp, lp, d)
    y_sample = x[tp:t].reshape(bs, ls, d)
    keep = min(LEFT_LEN, lp)
    if tsm % lp == 0:
        kvp = kv_f32.reshape(t // lp, lp, 2 * d)[:bp, lp - keep:]
    else:
        kvp = kv_f32[:tp].reshape(bp, lp, 2 * d)[:, lp - keep:]
    kvs = kv_f32[tp:].reshape(bs, ls, 2 * d)
    return (y_prompt, y_sample, jnp.stack(hist_p, axis=0), jnp.stack(hist_s, axis=0),
            kvp[..., :d].reshape(bp, keep, n_heads, hd), kvp[..., d:].reshape(bp, keep, n_heads, hd),
            kvs[..., :d].reshape(bs, ls, n_heads, hd), kvs[..., d:].reshape(bs, ls, n_heads, hd))
```

```python
import functools

import jax
import jax.numpy as jnp
from jax import lax
from jax.experimental import pallas as pl
from jax.experimental.pallas import tpu as pltpu

RMS_EPS = 1e-6
NEG_INF = -1e30
CHUNK = 64
LEFT_CHUNKS = 8
LEFT_LEN = LEFT_CHUNKS * CHUNK
POOL_WINDOWS = (2, 4, 8, 16)
POOL_HIST = max(POOL_WINDOWS) - 1
HALO = POOL_HIST + 1
PAST_LEN = 1024
LANES = 128
LOG2E = 1.4426950408889634

F32 = jnp.float32
BF16 = jnp.bfloat16


def _pick(n, cands):
    for c in cands:
        if n % c == 0:
            return c
    raise ValueError(f"no tile for {n} in {cands}")


def _params(sem, vmem_mib=None):
    return pltpu.CompilerParams(
        dimension_semantics=sem,
        vmem_limit_bytes=None if vmem_mib is None else vmem_mib * 1024 * 1024)


def _rms(x, g):
    ms = jnp.mean(x * x, axis=-1, keepdims=True)
    return (x * lax.rsqrt(ms + RMS_EPS)) * g


def _pool_kernel(*refs, ts, dg, pos0, aliased):
    if aliased:
        x_ref, halo0_ref, g_ref, w_ref, s_ref, _, out_ref, hist_ref, halo_sc = refs
    else:
        x_ref, halo0_ref, g_ref, w_ref, s_ref, out_ref, hist_ref, halo_sc = refs
    s = pl.program_id(1)
    ns = pl.num_programs(1)

    @pl.when(s == 0)
    def _():
        halo_sc[...] = halo0_ref[0]

    x = x_ref[...]
    h = _rms(x, g_ref[...])
    halo = halo_sc[...]
    pos = lax.broadcasted_iota(jnp.int32, (ts, 1), 0) + (s * ts + pos0)
    for gi, w in enumerate(POOL_WINDOWS):
        lo, hi = gi * dg, (gi + 1) * dg
        hg = h[:, lo:hi]
        acc = jnp.concatenate([halo[:, lo:hi], hg], axis=0)
        k = 1
        while k < w:
            acc = acc + pltpu.roll(acc, k, 0)
            k *= 2
        cnt = jnp.minimum(pos + 1, w).astype(F32)
        pooled = acc[HALO:, :] * (1.0 / cnt) - hg
        mix = jnp.dot(pooled.astype(BF16), w_ref[gi], preferred_element_type=F32) * s_ref[:, lo:hi]
        out_ref[:, lo:hi] = x[:, lo:hi] + mix
    halo_sc[...] = h[ts - HALO:, :]

    @pl.when(s == ns - 1)
    def _():
        hist_ref[0] = halo_sc[1:HALO, :]


def _pool_layer(x_in, in_off, out_buf, out_off, n_tok_total, batch, seq, halo0, g, w_bf, s, pos0):
    d = x_in.shape[1]
    dg = d // len(POOL_WINDOWS)
    ts = _pick(seq, (256, 128, 64))
    ns = seq // ts
    assert in_off % ts == 0 and out_off % ts == 0
    ib, ob = in_off // ts, out_off // ts
    aliased = out_buf is not None
    halo_b = halo0.shape[0]
    in_specs = [
        pl.BlockSpec((ts, d), lambda b, i: (ib + b * ns + i, 0)),
        pl.BlockSpec((1, HALO, d), (lambda b, i: (b, 0, 0)) if halo_b > 1 else (lambda b, i: (0, 0, 0))),
        pl.BlockSpec((1, d), lambda b, i: (0, 0)),
        pl.BlockSpec((len(POOL_WINDOWS), dg, dg), lambda b, i: (0, 0, 0)),
        pl.BlockSpec((1, d), lambda b, i: (0, 0)),
    ]
    args = [x_in, halo0, g.reshape(1, d), w_bf, s.reshape(1, d)]
    if aliased:
        in_specs.append(pl.BlockSpec(memory_space=pl.ANY))
        args.append(out_buf)
    return pl.pallas_call(
        functools.partial(_pool_kernel, ts=ts, dg=dg, pos0=pos0, aliased=aliased),
        grid=(batch, ns),
        in_specs=in_specs,
        out_specs=[
            pl.BlockSpec((ts, d), lambda b, i: (ob + b * ns + i, 0)),
            pl.BlockSpec((1, POOL_HIST, d), lambda b, i: (b, 0, 0)),
        ],
        out_shape=[
            jax.ShapeDtypeStruct((n_tok_total, d), F32),
            jax.ShapeDtypeStruct((batch, POOL_HIST, d), F32),
        ],
        scratch_shapes=[pltpu.VMEM((HALO, d), F32)],
        input_output_aliases={len(args) - 1: 0} if aliased else {},
        compiler_params=_params(("arbitrary", "arbitrary"), 48),
        name="pool_layer",
    )(*args)


def _router_kernel(x_ref, g_ref, w_ref, b_ref, route_ref, cnt_ref, xrow_ref, carry_sc, *, tm, n_groups, per_group):
    i = pl.program_id(0)

    @pl.when(i == 0)
    def _():
        carry_sc[...] = jnp.zeros_like(carry_sc)

    x = x_ref[...]
    nc = x.shape[1] // LANES
    for c in range(nc):
        xrow_ref[pl.ds(c, tm, stride=nc), :] = x[:, c * LANES:(c + 1) * LANES]

    h = _rms(x, g_ref[...])
    lg = jnp.dot(h.astype(BF16), w_ref[...], preferred_element_type=F32) + b_ref[...]
    lane = lax.broadcasted_iota(jnp.int32, (tm, LANES), 1).astype(F32)
    big = float(LANES)

    def first_max(vals, mask):
        v = jnp.max(jnp.where(mask, vals, -jnp.inf), axis=-1, keepdims=True)
        idx = jnp.min(jnp.where(mask & (vals == v), lane, big), axis=-1, keepdims=True)
        return v, idx

    gmask = lane < n_groups
    _, gsel = first_max(lg, gmask)
    lo = n_groups + gsel * per_group
    emask = (lane >= lo) & (lane < lo + per_group)
    _, i1 = first_max(lg, emask)
    _, i2 = first_max(lg, emask & (lane != i1))

    a = jnp.minimum(i1, i2) - lo
    b = jnp.maximum(i1, i2) - lo
    n_pairs = per_group * (per_group - 1) // 2
    key = gsel * n_pairs + a * (2 * per_group - 1 - a) * 0.5 + (b - a - 1.0)
    sel = lane == key
    onehot = jnp.where(sel, 1.0, 0.0)
    r_i = lax.broadcasted_iota(jnp.int32, (tm, tm), 0)
    c_i = lax.broadcasted_iota(jnp.int32, (tm, tm), 1)
    lower = jnp.where(c_i < r_i, 1.0, 0.0).astype(BF16)
    before = jnp.dot(lower, onehot.astype(BF16), preferred_element_type=F32) + carry_sc[...]
    rank = jnp.sum(jnp.where(sel, before, 0.0), axis=-1, keepdims=True)
    carry = carry_sc[...] + jnp.sum(onehot, axis=0, keepdims=True)
    carry_sc[...] = carry
    cnt_ref[...] = carry

    route_ref[...] = jnp.where(lane == 0, key, jnp.where(lane == 1, rank, 0.0))


def _route(x, t, g, w_rt, b_rt, n_groups, per_group):
    d = x.shape[1]
    tm = _pick(t, (512, 256, 128))
    return pl.pallas_call(
        functools.partial(_router_kernel, tm=tm, n_groups=n_groups, per_group=per_group),
        grid=(t // tm,),
        in_specs=[
            pl.BlockSpec((tm, d), lambda i: (i, 0)),
            pl.BlockSpec((1, d), lambda i: (0, 0)),
            pl.BlockSpec((d, LANES), lambda i: (0, 0)),
            pl.BlockSpec((1, LANES), lambda i: (0, 0)),
        ],
        out_specs=[
            pl.BlockSpec((tm, LANES), lambda i: (i, 0)),
            pl.BlockSpec((1, LANES), lambda i: (0, 0)),
            pl.BlockSpec((tm * (d // LANES), LANES), lambda i: (i, 0)),
        ],
        out_shape=[
            jax.ShapeDtypeStruct((t, LANES), F32),
            jax.ShapeDtypeStruct((1, LANES), F32),
            jax.ShapeDtypeStruct((t * (d // LANES), LANES), F32),
        ],
        scratch_shapes=[pltpu.VMEM((1, LANES), F32)],
        compiler_params=_params(("arbitrary",), 48),
        name="moe_route",
    )(x, g.reshape(1, d), w_rt, b_rt)


def _moe_kernel(ea_ref, eb_ref, nv_ref, srcc_ref, srcn_ref, dst_ref, x_hbm, g_ref, wrt_ref, brt_ref,
                wga_ref, wua_ref, wda_ref, wgb_ref, wub_ref, wdb_ref, *rest, tb, n_tok, n_groups, final):
    if final:
        gf_ref, out_hbm, xbuf, obuf, gsem, ssem = rest
    else:
        out_hbm, xbuf, obuf, gsem, ssem = rest
    b = pl.program_id(0)
    nb = pl.num_programs(0)
    slot = lax.rem(b, 2)
    nv = nv_ref[b]
    nv_next = nv_ref[jnp.minimum(b + 1, nb - 1)]
    nv_prev2 = nv_ref[jnp.maximum(b - 2, 0)]
    nv_prev1 = nv_ref[jnp.maximum(b - 1, 0)]

    nc = xbuf.shape[1] // tb

    def gather(idx_ref, s):
        for r in range(tb):
            src = x_hbm.at[pl.ds(pl.multiple_of(idx_ref[0, 0, r] * nc, nc), nc), :]
            pltpu.make_async_copy(src, xbuf.at[s, pl.ds(r * nc, nc), :], gsem.at[s]).start()

    def gather_wait(s):
        pltpu.make_async_copy(x_hbm.at[pl.ds(0, tb * nc), :], xbuf.at[s], gsem.at[s]).wait()

    def scatter_wait(s):
        pltpu.make_async_copy(obuf.at[s], out_hbm.at[pl.ds(0, tb), :], ssem.at[s]).wait()

    @pl.when(b == 0)
    def _():
        obuf[...] = jnp.zeros_like(obuf)
        for s in range(2):
            spare = pltpu.make_async_copy(obuf.at[s], out_hbm.at[pl.ds(n_tok + s * tb, tb), :], ssem.at[s])
            spare.start()
            spare.wait()

    @pl.when((b == 0) & (nv > 0))
    def _():
        gather(srcc_ref, 0)

    @pl.when((b + 1 < nb) & (nv_next > 0))
    def _():
        gather(srcn_ref, 1 - slot)

    @pl.when((b >= 2) & (nv_prev2 > 0))
    def _():
        scatter_wait(slot)

    @pl.when(nv > 0)
    def _():
        gather_wait(slot)
        xrows = xbuf.at[slot]
        x = jnp.concatenate([xrows[pl.ds(c, tb, stride=nc), :] for c in range(nc)], axis=1)
        hn = _rms(x, g_ref[...]).astype(BF16)

        def expert(wg_ref, wu_ref, wd_ref):
            a = jnp.dot(hn, wg_ref[...], preferred_element_type=F32)
            u = jnp.dot(hn, wu_ref[...], preferred_element_type=F32)
            act = (a * (1.0 / (1.0 + jnp.exp(-a)))) * u
            return jnp.dot(act.astype(BF16), wd_ref[...], preferred_element_type=F32)

        lg = jnp.dot(hn, wrt_ref[...], preferred_element_type=F32) + brt_ref[...]
        lane = lax.broadcasted_iota(jnp.int32, (tb, LANES), 1)
        gmask = lane < n_groups
        gmax = jnp.max(jnp.where(gmask, lg, -jnp.inf), axis=-1, keepdims=True)
        g_prob = 1.0 / jnp.sum(jnp.where(gmask, jnp.exp(lg - gmax), 0.0), axis=-1, keepdims=True)
        la = jnp.sum(jnp.where(lane == n_groups + ea_ref[b], lg, 0.0), axis=-1, keepdims=True)
        lb = jnp.sum(jnp.where(lane == n_groups + eb_ref[b], lg, 0.0), axis=-1, keepdims=True)
        top = jnp.maximum(la, lb)
        pa = jnp.exp(la - top)
        pb = jnp.exp(lb - top)
        den = pa + pb
        gate_a = (pa / den) * g_prob
        gate_b = (pb / den) * g_prob

        ya = expert(wga_ref, wua_ref, wda_ref)
        yb = expert(wgb_ref, wub_ref, wdb_ref)
        out = x + (gate_a * ya + gate_b * yb)
        if final:
            out = _rms(out, gf_ref[...])
        obuf[slot] = out

        for r in range(tb):
            pltpu.make_async_copy(obuf.at[slot, pl.ds(r, 1), :], out_hbm.at[pl.ds(dst_ref[0, 0, r], 1), :],
                                  ssem.at[slot]).start()

    @pl.when(b == nb - 1)
    def _():
        @pl.when((b >= 1) & (nv_prev1 > 0))
        def _():
            scatter_wait(1 - slot)

        @pl.when(nv > 0)
        def _():
            scatter_wait(slot)


def _moe_mlp(xrow, t, g, w_rt, b_rt, n_groups, wg, wu, wd, layer, blk_ea, blk_eb, blk_nv, slot_src, slot_dst, tb,
             g_final):
    x = xrow
    d = wg.shape[-2]
    de = wg.shape[-1]
    nb = blk_nv.shape[0]
    src3 = slot_src.reshape(nb, 1, tb)
    dst3 = slot_dst.reshape(nb, 1, tb)
    final = g_final is not None

    def wspec(which, shape):
        return pl.BlockSpec((None, None) + shape, lambda b, ea, eb, nv: (layer, (ea, eb)[which][b], 0, 0))

    in_specs = [
        pl.BlockSpec((1, 1, tb), lambda b, ea, eb, nv: (b, 0, 0), memory_space=pltpu.SMEM),
        pl.BlockSpec((1, 1, tb), lambda b, ea, eb, nv: (jnp.minimum(b + 1, nb - 1), 0, 0), memory_space=pltpu.SMEM),
        pl.BlockSpec((1, 1, tb), lambda b, ea, eb, nv: (b, 0, 0), memory_space=pltpu.SMEM),
        pl.BlockSpec(memory_space=pl.ANY),
        pl.BlockSpec((1, d), lambda b, ea, eb, nv: (0, 0)),
        pl.BlockSpec((d, LANES), lambda b, ea, eb, nv: (0, 0)),
        pl.BlockSpec((1, LANES), lambda b, ea, eb, nv: (0, 0)),
        wspec(0, (d, de)), wspec(0, (d, de)), wspec(0, (de, d)),
        wspec(1, (d, de)), wspec(1, (d, de)), wspec(1, (de, d)),
    ]
    args = [blk_ea, blk_eb, blk_nv, src3, src3, dst3, x, g.reshape(1, d), w_rt, b_rt, wg, wu, wd, wg, wu, wd]
    if final:
        in_specs.append(pl.BlockSpec((1, d), lambda b, ea, eb, nv: (0, 0)))
        args.append(g_final.reshape(1, d))
    grid_spec = pltpu.PrefetchScalarGridSpec(
        num_scalar_prefetch=3,
        grid=(nb,),
        in_specs=in_specs,
        out_specs=pl.BlockSpec(memory_space=pl.ANY),
        scratch_shapes=[pltpu.VMEM((2, tb * (d // LANES), LANES), F32), pltpu.VMEM((2, tb, d), F32),
                        pltpu.SemaphoreType.DMA((2,)), pltpu.SemaphoreType.DMA((2,))],
    )
    return pl.pallas_call(
        functools.partial(_moe_kernel, tb=tb, n_tok=t, n_groups=n_groups, final=final),
        grid_spec=grid_spec,
        out_shape=jax.ShapeDtypeStruct((t + 2 * tb, d), F32),
        compiler_params=_params(("arbitrary",), 56),
        name="moe_mlp",
    )(*args)


def _pair_tables(n_groups, per_group):
    ea, eb = [], []
    for grp in range(n_groups):
        for a in range(per_group):
            for b in range(a + 1, per_group):
                ea.append(grp * per_group + a)
                eb.append(grp * per_group + b)
    return jnp.asarray(ea, jnp.int32), jnp.asarray(eb, jnp.int32)


def _hier_moe(x, t, layer, p, tb, g_final=None):
    n_groups, n_experts = p["n_groups"], p["n_experts"]
    per_group = n_experts // n_groups
    n_buckets = n_groups * (per_group * (per_group - 1) // 2)
    assert n_buckets <= LANES
    route, cnt, xrow = _route(x, t, p["g_ffn"][layer], p["w_rt"][layer], p["b_rt"][layer], n_groups, per_group)

    key = route[:, 0].astype(jnp.int32)
    rank = route[:, 1].astype(jnp.int32)
    counts = cnt[0, :n_buckets].astype(jnp.int32)
    padded = (counts + tb - 1) // tb * tb
    pend = jnp.cumsum(padded)
    pstart = pend - padded
    bucket_ids = jnp.arange(n_buckets, dtype=jnp.int32)
    dest = jnp.sum(jnp.where(key[:, None] == bucket_ids[None, :], pstart[None, :], 0), axis=1) + rank
    n_blocks = -(-(t + n_buckets * (tb - 1)) // tb)
    n_slots = n_blocks * tb
    blk_start = jnp.arange(n_blocks, dtype=jnp.int32) * tb
    blk_bucket = jnp.minimum(jnp.sum((pend[None, :] <= blk_start[:, None]).astype(jnp.int32), axis=1), n_buckets - 1)
    blk_nv = jnp.clip(counts[blk_bucket] - (blk_start - pstart[blk_bucket]), 0, tb).astype(jnp.int32)
    tab_a, tab_b = _pair_tables(n_groups, per_group)
    blk_ea, blk_eb = tab_a[blk_bucket], tab_b[blk_bucket]

    slot_tok = jnp.zeros((n_slots,), jnp.int32).at[dest].set(jnp.arange(t, dtype=jnp.int32))
    slot = jnp.arange(n_slots, dtype=jnp.int32)
    in_blk, blk = slot % tb, slot // tb
    valid = in_blk < jnp.repeat(blk_nv, tb)
    slot_dst = jnp.where(valid, slot_tok, t + (blk % 2) * tb + in_blk)

    return _moe_mlp(xrow, t, p["g_ffn"][layer], p["w_rt"][layer], p["b_rt"][layer], n_groups, p["w_gate"], p["w_up"],
                    p["w_down"], layer, blk_ea, blk_eb, blk_nv, slot_tok, slot_dst, tb, g_final)


def _rms_proj_kernel(x_ref, g_ref, w_ref, *out_refs):
    res = jnp.dot(_rms(x_ref[...], g_ref[...]).astype(BF16), w_ref[...], preferred_element_type=F32)
    for o in out_refs:
        o[...] = res.astype(o.dtype)


def _proj_tiles(t, n):
    return _pick(t, (512, 256, 128)), _pick(n, (2048, 1024, 512, 256, 128))


def _rms_proj(x, t, g, w_bf, out_dtypes):
    d = x.shape[1]
    n = w_bf.shape[1]
    tm, tn = _proj_tiles(t, n)
    return pl.pallas_call(
        _rms_proj_kernel,
        grid=(n // tn, t // tm),
        in_specs=[
            pl.BlockSpec((tm, d), lambda j, i: (i, 0)),
            pl.BlockSpec((1, d), lambda j, i: (0, 0)),
            pl.BlockSpec((d, tn), lambda j, i: (0, j)),
        ],
        out_specs=[pl.BlockSpec((tm, tn), lambda j, i: (i, j)) for _ in out_dtypes],
        out_shape=[jax.ShapeDtypeStruct((t, n), dt) for dt in out_dtypes],
        compiler_params=_params(("parallel", "parallel"), 56),
        name="rms_proj",
    )(x, g.reshape(1, d), w_bf)


def _proj_res_kernel(o_ref, w_ref, x_ref, out_ref):
    out_ref[...] = x_ref[...] + jnp.dot(o_ref[...], w_ref[...], preferred_element_type=F32)


def _proj_res(o_bf, w_bf, x, t):
    d = x.shape[1]
    k = o_bf.shape[1]
    tm, tn = _proj_tiles(t, d)
    return pl.pallas_call(
        _proj_res_kernel,
        grid=(d // tn, t // tm),
        in_specs=[
            pl.BlockSpec((tm, k), lambda j, i: (i, 0)),
            pl.BlockSpec((k, tn), lambda j, i: (0, j)),
            pl.BlockSpec((tm, tn), lambda j, i: (i, j)),
        ],
        out_specs=pl.BlockSpec((tm, tn), lambda j, i: (i, j)),
        out_shape=jax.ShapeDtypeStruct((t, d), F32),
        compiler_params=_params(("parallel", "parallel"), 56),
        name="proj_res",
    )(o_bf, w_bf, x)


def _attn_kernel(*refs, n_seg, n_heads, hd, scale, clamp_mask):
    q_ref = refs[0]
    kv_refs = refs[1:1 + 2 * n_seg]
    bias_ref = refs[1 + 2 * n_seg]
    o_ref = refs[-1]
    i = pl.program_id(1)

    def seg(ref, h):
        sl = slice(h * hd, (h + 1) * hd)
        return ref[0, :, sl] if len(ref.shape) == 3 else ref[:, sl]

    for h in range(n_heads):
        sl = slice(h * hd, (h + 1) * hd)
        qh = q_ref[:, sl]
        scores = []
        off = 0
        for si in range(n_seg):
            kh = seg(kv_refs[2 * si], h)
            nk = kh.shape[0]
            s = lax.dot_general(qh, kh, (((1,), (1,)), ((), ())), preferred_element_type=F32) * scale
            s = s + bias_ref[h, :, off:off + nk]
            if clamp_mask and si < n_seg - 1:
                s = jnp.where(i >= n_seg - 1 - si, s, NEG_INF)
            scores.append(s)
            off += nk
        m = functools.reduce(jnp.maximum, [jnp.max(s, axis=-1, keepdims=True) for s in scores])
        ps = [jnp.exp2(s - m) for s in scores]
        l = functools.reduce(lambda a, b: a + b, [jnp.sum(p, axis=-1, keepdims=True) for p in ps])
        acc = None
        for si in range(n_seg):
            pv = jnp.dot(ps[si].astype(BF16), seg(kv_refs[2 * si + 1], h), preferred_element_type=F32)
            acc = pv if acc is None else acc + pv
        o_ref[:, sl] = (acc / l).astype(o_ref.dtype)


def _band_bias(table, qb, n_keys, left):
    clip = (table.shape[-1] - 1) // 2
    n_heads = table.shape[0]
    w = qb + n_keys
    m = jnp.arange(w, dtype=jnp.int32)
    delta = jnp.where(m < n_keys, m, m - w)
    u = table[:, jnp.clip(left - delta, -clip, clip) + clip].astype(F32)
    toe = jnp.tile(u, (1, qb))[:, :qb * (w - 1)].reshape(n_heads, qb, w - 1)[:, :, :n_keys]
    i = jnp.arange(qb, dtype=jnp.int32)[:, None]
    j = jnp.arange(n_keys, dtype=jnp.int32)[None, :]
    kpos = j - left
    cq = (i // CHUNK) * CHUNK
    inband = (kpos >= cq - LEFT_LEN) & (kpos < cq + CHUNK)
    return jnp.where(inband[None], toe * LOG2E, NEG_INF)


def _attn_prompt(q, kv, table, batch, seq, n_heads, n_tok_total):
    d = q.shape[1]
    hd = d // n_heads
    qb = _pick(seq, (256, 128, 64))
    nq = seq // qb
    n_seg = LEFT_LEN // qb + 1
    bias = _band_bias(table, qb, n_seg * qb, LEFT_LEN)

    def kv_spec(si, col):
        back = n_seg - 1 - si
        return pl.BlockSpec((qb, d), lambda b, i: (b * nq + jnp.maximum(i - back, 0), col))

    in_specs = [pl.BlockSpec((qb, d), lambda b, i: (b * nq + i, 0))]
    args = [q]
    for si in range(n_seg):
        in_specs += [kv_spec(si, 0), kv_spec(si, 1)]
        args += [kv, kv]
    in_specs.append(pl.BlockSpec((n_heads, qb, n_seg * qb), lambda b, i: (0, 0, 0)))
    args.append(bias)
    return pl.pallas_call(
        functools.partial(_attn_kernel, n_seg=n_seg, n_heads=n_heads, hd=hd, scale=hd ** -0.5 * LOG2E,
                          clamp_mask=True),
        grid=(batch, nq),
        in_specs=in_specs,
        out_specs=pl.BlockSpec((qb, d), lambda b, i: (b * nq + i, 0)),
        out_shape=jax.ShapeDtypeStruct((n_tok_total, d), BF16),
        compiler_params=_params(("parallel", "arbitrary"), 56),
        name="attn_prompt",
    )(*args)


def _attn_sample_kernel(*refs, **kw):
    _attn_kernel(*refs[:-2], refs[-1], **kw)


def _attn_sample(q, kv, cache_k, cache_v, table, row0, batch, seq, n_heads, o_buf):
    d = q.shape[1]
    hd = d // n_heads
    past = cache_k.shape[1]
    assert row0 % seq == 0
    rb = row0 // seq
    assert seq <= CHUNK and past == LEFT_LEN
    bias = _band_bias(table, seq, past + seq, past)
    in_specs = [
        pl.BlockSpec((seq, d), lambda b, i: (rb + b, 0)),
        pl.BlockSpec((1, past, d), lambda b, i: (b, 0, 0)),
        pl.BlockSpec((1, past, d), lambda b, i: (b, 0, 0)),
        pl.BlockSpec((seq, d), lambda b, i: (rb + b, 0)),
        pl.BlockSpec((seq, d), lambda b, i: (rb + b, 1)),
        pl.BlockSpec((n_heads, seq, past + seq), lambda b, i: (0, 0, 0)),
        pl.BlockSpec(memory_space=pl.ANY),
    ]
    return pl.pallas_call(
        functools.partial(_attn_sample_kernel, n_seg=2, n_heads=n_heads, hd=hd, scale=hd ** -0.5 * LOG2E,
                          clamp_mask=False),
        grid=(batch, 1),
        in_specs=in_specs,
        out_specs=pl.BlockSpec((seq, d), lambda b, i: (rb + b, 0)),
        out_shape=jax.ShapeDtypeStruct(o_buf.shape, BF16),
        input_output_aliases={6: 0},
        compiler_params=_params(("parallel", "arbitrary"), 56),
        name="attn_sample",
    )(q, cache_k, cache_v, kv, kv, bias, o_buf)


def kernel(x_prompt, x_sample, state_pool, cache_k, cache_v, g_pool, w_pool, s_pool, g_kv, w_kv, g_attn, w_q, w_o, rel_bias, g_ffn, w_group, b_group, w_router, b_router, w_gate, w_up, w_down, g_final):
    bp, lp, d = x_prompt.shape
    bs, ls, _ = x_sample.shape
    depth = g_ffn.shape[0]
    n_pool = g_pool.shape[0]
    n_heads, hd = cache_k.shape[2], cache_k.shape[3]
    n_groups, n_experts = w_group.shape[-1], w_router.shape[-1]
    tp, tsm = bp * lp, bs * ls
    t = tp + tsm
    tb = 128

    w_rt = jnp.concatenate([w_group, w_router], axis=-1)
    w_rt = jnp.pad(w_rt, ((0, 0), (0, 0), (0, LANES - w_rt.shape[-1]))).astype(BF16)
    b_rt = jnp.concatenate([b_group, b_router], axis=-1)
    b_rt = jnp.pad(b_rt, ((0, 0), (0, LANES - b_rt.shape[-1])))[:, None, :]
    moe_p = dict(n_groups=n_groups, n_experts=n_experts, g_ffn=g_ffn, w_rt=w_rt, b_rt=b_rt,
                 w_gate=w_gate.astype(BF16), w_up=w_up.astype(BF16), w_down=w_down.astype(BF16))
    w_pool_bf = w_pool.astype(BF16)
    w_kv_bf = w_kv.astype(BF16)
    w_q_bf = w_q.astype(BF16)
    w_o_bf = w_o.astype(BF16)

    zero_halo = jnp.zeros((1, HALO, d), F32)
    samp_halo = jnp.pad(state_pool, ((0, 0), (0, 0), (1, 0), (0, 0)))

    ck = cache_k.reshape(bs, cache_k.shape[1], d).astype(BF16)
    cv = cache_v.reshape(bs, cache_v.shape[1], d).astype(BF16)

    x = None
    hist_p, hist_s = [], []
    kv_f32 = kv_bf = None
    for layer in range(depth):
        if layer < n_pool:
            if layer == 0:
                src_p, off_p, src_s, off_s = x_prompt.reshape(tp, d), 0, x_sample.reshape(tsm, d), 0
            else:
                src_p, off_p, src_s, off_s = x, 0, x, tp
            buf, hp = _pool_layer(src_p, off_p, None, 0, t, bp, lp, zero_halo, g_pool[layer], w_pool_bf[layer],
                                  s_pool[layer], 0)
            x, hs = _pool_layer(src_s, off_s, buf, tp, t, bs, ls, samp_halo[layer], g_pool[layer], w_pool_bf[layer],
                                s_pool[layer], PAST_LEN)
            hist_p.append(hp)
            hist_s.append(hs)
        else:
            j = layer - n_pool
            if j == 0:
                kv_f32, kv_bf = _rms_proj(x, t, g_kv, w_kv_bf, (F32, BF16))
            (q,) = _rms_proj(x, t, g_attn[j], w_q_bf[j], (BF16,))
            o = _attn_prompt(q, kv_bf, rel_bias[j], bp, lp, n_heads, t)
            o = _attn_sample(q, kv_bf, ck, cv, rel_bias[j], tp, bs, ls, n_heads, o)
            x = _proj_res(o, w_o_bf[j], x, t)
        x = _hier_moe(x, t, layer, moe_p, tb, g_final if layer == depth - 1 else None)

    y_prompt = x[:tp].reshape(bp, lp, d)
    y_sample = x[tp:t].reshape(bs, ls, d)
    keep = min(LEFT_LEN, lp)
    if tsm % lp == 0:
        kvp = kv_f32.reshape(t // lp, lp, 2 * d)[:bp, lp - keep:]
    else:
        kvp = kv_f32[:tp].reshape(bp, lp, 2 * d)[:, lp - keep:]
    kvs = kv_f32[tp:].reshape(bs, ls, 2 * d)
    return (y_prompt, y_sample, jnp.stack(hist_p, axis=0), jnp.stack(hist_s, axis=0),
            kvp[..., :d].reshape(bp, keep, n_heads, hd), kvp[..., d:].reshape(bp, keep, n_heads, hd),
            kvs[..., :d].reshape(bs, ls, n_heads, hd), kvs[..., d:].reshape(bs, ls, n_heads, hd))
```

```python
import functools

import jax
import jax.numpy as jnp
from jax import lax
from jax.experimental import pallas as pl
from jax.experimental.pallas import tpu as pltpu

RMS_EPS = 1e-6
NEG_INF = -1e30
CHUNK = 64
LEFT_CHUNKS = 8
LEFT_LEN = LEFT_CHUNKS * CHUNK
POOL_WINDOWS = (2, 4, 8, 16)
POOL_HIST = max(POOL_WINDOWS) - 1
HALO = POOL_HIST + 1
PAST_LEN = 1024
LANES = 128
LOG2E = 1.4426950408889634

F32 = jnp.float32
BF16 = jnp.bfloat16


def _pick(n, cands):
    for c in cands:
        if n % c == 0:
            return c
    raise ValueError(f"no tile for {n} in {cands}")


def _params(sem, vmem_mib=None):
    return pltpu.CompilerParams(
        dimension_semantics=sem,
        vmem_limit_bytes=None if vmem_mib is None else vmem_mib * 1024 * 1024)


def _rms(x, g):
    ms = jnp.mean(x * x, axis=-1, keepdims=True)
    return (x * lax.rsqrt(ms + RMS_EPS)) * g


def _pool_kernel(*refs, ts, ns, n_real, dg, pos0, aliased):
    if aliased:
        x_ref, halo0_ref, g_ref, w_ref, s_ref, _, out_ref, hist_ref, halo_sc = refs
    else:
        x_ref, halo0_ref, g_ref, w_ref, s_ref, out_ref, hist_ref, halo_sc = refs
    step = pl.program_id(0)
    s = lax.rem(step, ns)

    @pl.when(step >= n_real)
    def _():
        out_ref[...] = jnp.zeros_like(out_ref)

    @pl.when(step < n_real)
    def _():
        @pl.when(s == 0)
        def _():
            halo_sc[...] = halo0_ref[0]

        x = x_ref[...]
        h = _rms(x, g_ref[...])
        halo = halo_sc[...]
        pos = lax.broadcasted_iota(jnp.int32, (ts, 1), 0) + (s * ts + pos0)
        for gi, w in enumerate(POOL_WINDOWS):
            lo, hi = gi * dg, (gi + 1) * dg
            hg = h[:, lo:hi]
            acc = jnp.concatenate([halo[:, lo:hi], hg], axis=0)
            k = 1
            while k < w:
                acc = acc + pltpu.roll(acc, k, 0)
                k *= 2
            cnt = jnp.minimum(pos + 1, w).astype(F32)
            pooled = acc[HALO:, :] * (1.0 / cnt) - hg
            mix = jnp.dot(pooled.astype(BF16), w_ref[gi], preferred_element_type=F32) * s_ref[:, lo:hi]
            out_ref[:, lo:hi] = x[:, lo:hi] + mix
        halo_sc[...] = h[ts - HALO:, :]

        @pl.when(s == ns - 1)
        def _():
            hist_ref[0] = halo_sc[1:HALO, :]


def _pool_layer(x_in, in_off, out_buf, out_off, n_tok_total, batch, seq, halo0, g, w_bf, s, pos0, n_fill_rows=0):
    d = x_in.shape[1]
    dg = d // len(POOL_WINDOWS)
    ts = _pick(seq, (256, 128, 64))
    ns = seq // ts
    assert in_off % ts == 0 and out_off % ts == 0 and n_fill_rows % ts == 0
    ib, ob = in_off // ts, out_off // ts
    n_real = batch * ns
    aliased = out_buf is not None
    halo_b = halo0.shape[0]

    def row(i):
        return jnp.minimum(i, n_real - 1) // ns

    in_specs = [
        pl.BlockSpec((ts, d), lambda i: (ib + jnp.minimum(i, n_real - 1), 0)),
        pl.BlockSpec((1, HALO, d), (lambda i: (row(i), 0, 0)) if halo_b > 1 else (lambda i: (0, 0, 0))),
        pl.BlockSpec((1, d), lambda i: (0, 0)),
        pl.BlockSpec((len(POOL_WINDOWS), dg, dg), lambda i: (0, 0, 0)),
        pl.BlockSpec((1, d), lambda i: (0, 0)),
    ]
    args = [x_in, halo0, g.reshape(1, d), w_bf, s.reshape(1, d)]
    if aliased:
        in_specs.append(pl.BlockSpec(memory_space=pl.ANY))
        args.append(out_buf)
    return pl.pallas_call(
        functools.partial(_pool_kernel, ts=ts, ns=ns, n_real=n_real, dg=dg, pos0=pos0, aliased=aliased),
        grid=(n_real + n_fill_rows // ts,),
        in_specs=in_specs,
        out_specs=[
            pl.BlockSpec((ts, d), lambda i: (ob + i, 0)),
            pl.BlockSpec((1, POOL_HIST, d), lambda i: (row(i), 0, 0)),
        ],
        out_shape=[
            jax.ShapeDtypeStruct((n_tok_total, d), F32),
            jax.ShapeDtypeStruct((batch, POOL_HIST, d), F32),
        ],
        scratch_shapes=[pltpu.VMEM((HALO, d), F32)],
        input_output_aliases={len(args) - 1: 0} if aliased else {},
        compiler_params=_params(("arbitrary",), 48),
        name="pool_layer",
    )(*args)


def _router_kernel(x_ref, g_ref, w_ref, b_ref, route_ref, cnt_ref, xrow_ref, carry_sc, *, tm, n_groups, per_group):
    i = pl.program_id(0)

    @pl.when(i == 0)
    def _():
        carry_sc[...] = jnp.zeros_like(carry_sc)

    x = x_ref[...]
    nc = x.shape[1] // LANES
    for c in range(nc):
        xrow_ref[pl.ds(c, tm, stride=nc), :] = x[:, c * LANES:(c + 1) * LANES]

    h = _rms(x, g_ref[...])
    lg = jnp.dot(h.astype(BF16), w_ref[...], preferred_element_type=F32) + b_ref[...]
    lane = lax.broadcasted_iota(jnp.int32, (tm, LANES), 1).astype(F32)
    big = float(LANES)

    def first_max(vals, mask):
        v = jnp.max(jnp.where(mask, vals, -jnp.inf), axis=-1, keepdims=True)
        idx = jnp.min(jnp.where(mask & (vals == v), lane, big), axis=-1, keepdims=True)
        return v, idx

    gmask = lane < n_groups
    _, gsel = first_max(lg, gmask)
    lo = n_groups + gsel * per_group
    emask = (lane >= lo) & (lane < lo + per_group)
    _, i1 = first_max(lg, emask)
    _, i2 = first_max(lg, emask & (lane != i1))

    a = jnp.minimum(i1, i2) - lo
    b = jnp.maximum(i1, i2) - lo
    n_pairs = per_group * (per_group - 1) // 2
    key = gsel * n_pairs + a * (2 * per_group - 1 - a) * 0.5 + (b - a - 1.0)
    sel = lane == key
    onehot = jnp.where(sel, 1.0, 0.0)
    r_i = lax.broadcasted_iota(jnp.int32, (tm, tm), 0)
    c_i = lax.broadcasted_iota(jnp.int32, (tm, tm), 1)
    lower = jnp.where(c_i < r_i, 1.0, 0.0).astype(BF16)
    before = jnp.dot(lower, onehot.astype(BF16), preferred_element_type=F32) + carry_sc[...]
    rank = jnp.sum(jnp.where(sel, before, 0.0), axis=-1, keepdims=True)
    carry = carry_sc[...] + jnp.sum(onehot, axis=0, keepdims=True)
    carry_sc[...] = carry
    cnt_ref[...] = carry

    route_ref[...] = jnp.where(lane == 0, key, jnp.where(lane == 1, rank, 0.0))


def _route(x, t, g, w_rt, b_rt, n_groups, per_group):
    d = x.shape[1]
    tm = _pick(t, (512, 256, 128))
    return pl.pallas_call(
        functools.partial(_router_kernel, tm=tm, n_groups=n_groups, per_group=per_group),
        grid=(t // tm,),
        in_specs=[
            pl.BlockSpec((tm, d), lambda i: (i, 0)),
            pl.BlockSpec((1, d), lambda i: (0, 0)),
            pl.BlockSpec((d, LANES), lambda i: (0, 0)),
            pl.BlockSpec((1, LANES), lambda i: (0, 0)),
        ],
        out_specs=[
            pl.BlockSpec((tm, LANES), lambda i: (i, 0)),
            pl.BlockSpec((1, LANES), lambda i: (0, 0)),
            pl.BlockSpec((tm * (d // LANES), LANES), lambda i: (i, 0)),
        ],
        out_shape=[
            jax.ShapeDtypeStruct((t, LANES), F32),
            jax.ShapeDtypeStruct((1, LANES), F32),
            jax.ShapeDtypeStruct((t * (d // LANES), LANES), F32),
        ],
        scratch_shapes=[pltpu.VMEM((1, LANES), F32)],
        compiler_params=_params(("arbitrary",), 48),
        name="moe_route",
    )(x, g.reshape(1, d), w_rt, b_rt)


def _moe_kernel(ea_ref, eb_ref, nv_ref, srcc_ref, srcn_ref, dstp_ref, dstc_ref, x_hbm, g_ref, wrt_ref, brt_ref,
                wga_ref, wua_ref, wda_ref, wgb_ref, wub_ref, wdb_ref, *rest, tb, n_tok, n_groups, final):
    if final:
        gf_ref, out_hbm, xbuf, obuf, gsem, ssem = rest
    else:
        out_hbm, xbuf, obuf, gsem, ssem = rest
    b = pl.program_id(0)
    nb = pl.num_programs(0)
    slot = lax.rem(b, 2)
    nv = nv_ref[b]
    nv_next = nv_ref[jnp.minimum(b + 1, nb - 1)]
    nv_prev2 = nv_ref[jnp.maximum(b - 2, 0)]
    nv_prev1 = nv_ref[jnp.maximum(b - 1, 0)]

    nc = xbuf.shape[1] // tb

    def gather(idx_ref, s):
        for r in range(tb):
            src = x_hbm.at[pl.ds(pl.multiple_of(idx_ref[0, 0, r] * nc, nc), nc), :]
            pltpu.make_async_copy(src, xbuf.at[s, pl.ds(r * nc, nc), :], gsem.at[s]).start()

    def gather_wait(s):
        pltpu.make_async_copy(x_hbm.at[pl.ds(0, tb * nc), :], xbuf.at[s], gsem.at[s]).wait()

    def scatter_wait(s):
        pltpu.make_async_copy(obuf.at[s], out_hbm.at[pl.ds(0, tb), :], ssem.at[s]).wait()

    @pl.when(b == 0)
    def _():
        obuf[...] = jnp.zeros_like(obuf)
        for s in range(2):
            spare = pltpu.make_async_copy(obuf.at[s], out_hbm.at[pl.ds(n_tok + s * tb, tb), :], ssem.at[s])
            spare.start()
            spare.wait()

    def scatter(idx_ref, s):
        for r in range(tb):
            pltpu.make_async_copy(obuf.at[s, pl.ds(r, 1), :], out_hbm.at[pl.ds(idx_ref[0, 0, r], 1), :],
                                  ssem.at[s]).start()

    @pl.when((b >= 1) & (nv_prev1 > 0))
    def _():
        scatter(dstp_ref, 1 - slot)

    @pl.when((b == 0) & (nv > 0))
    def _():
        gather(srcc_ref, 0)

    @pl.when((b + 1 < nb) & (nv_next > 0))
    def _():
        gather(srcn_ref, 1 - slot)

    @pl.when((b >= 2) & (nv_prev2 > 0))
    def _():
        scatter_wait(slot)

    @pl.when(nv > 0)
    def _():
        gather_wait(slot)
        xrows = xbuf.at[slot]
        x = jnp.concatenate([xrows[pl.ds(c, tb, stride=nc), :] for c in range(nc)], axis=1)
        hn = _rms(x, g_ref[...]).astype(BF16)

        def expert(wg_ref, wu_ref, wd_ref):
            a = jnp.dot(hn, wg_ref[...], preferred_element_type=F32)
            u = jnp.dot(hn, wu_ref[...], preferred_element_type=F32)
            act = (a * (1.0 / (1.0 + jnp.exp(-a)))) * u
            return jnp.dot(act.astype(BF16), wd_ref[...], preferred_element_type=F32)

        lg = jnp.dot(hn, wrt_ref[...], preferred_element_type=F32) + brt_ref[...]
        lane = lax.broadcasted_iota(jnp.int32, (tb, LANES), 1)
        gmask = lane < n_groups
        gmax = jnp.max(jnp.where(gmask, lg, -jnp.inf), axis=-1, keepdims=True)
        g_prob = 1.0 / jnp.sum(jnp.where(gmask, jnp.exp(lg - gmax), 0.0), axis=-1, keepdims=True)
        la = jnp.sum(jnp.where(lane == n_groups + ea_ref[b], lg, 0.0), axis=-1, keepdims=True)
        lb = jnp.sum(jnp.where(lane == n_groups + eb_ref[b], lg, 0.0), axis=-1, keepdims=True)
        top = jnp.maximum(la, lb)
        pa = jnp.exp(la - top)
        pb = jnp.exp(lb - top)
        den = pa + pb
        gate_a = (pa / den) * g_prob
        gate_b = (pb / den) * g_prob

        ya = expert(wga_ref, wua_ref, wda_ref)
        yb = expert(wgb_ref, wub_ref, wdb_ref)
        out = x + (gate_a * ya + gate_b * yb)
        if final:
            out = _rms(out, gf_ref[...])
        obuf[slot] = out

    @pl.when(b == nb - 1)
    def _():
        @pl.when(nv > 0)
        def _():
            scatter(dstc_ref, slot)
            scatter_wait(slot)

        @pl.when((b >= 1) & (nv_prev1 > 0))
        def _():
            scatter_wait(1 - slot)


def _moe_mlp(xrow, t, g, w_rt, b_rt, n_groups, wg, wu, wd, layer, blk_ea, blk_eb, blk_nv, slot_src, slot_dst, tb,
             g_final):
    x = xrow
    d = wg.shape[-2]
    de = wg.shape[-1]
    nb = blk_nv.shape[0]
    src3 = slot_src.reshape(nb, 1, tb)
    dst3 = slot_dst.reshape(nb, 1, tb)
    final = g_final is not None

    def wspec(which, shape):
        return pl.BlockSpec((None, None) + shape, lambda b, ea, eb, nv: (layer, (ea, eb)[which][b], 0, 0))

    in_specs = [
        pl.BlockSpec((1, 1, tb), lambda b, ea, eb, nv: (b, 0, 0), memory_space=pltpu.SMEM),
        pl.BlockSpec((1, 1, tb), lambda b, ea, eb, nv: (jnp.minimum(b + 1, nb - 1), 0, 0), memory_space=pltpu.SMEM),
        pl.BlockSpec((1, 1, tb), lambda b, ea, eb, nv: (jnp.maximum(b - 1, 0), 0, 0), memory_space=pltpu.SMEM),
        pl.BlockSpec((1, 1, tb), lambda b, ea, eb, nv: (b, 0, 0), memory_space=pltpu.SMEM),
        pl.BlockSpec(memory_space=pl.ANY),
        pl.BlockSpec((1, d), lambda b, ea, eb, nv: (0, 0)),
        pl.BlockSpec((d, LANES), lambda b, ea, eb, nv: (0, 0)),
        pl.BlockSpec((1, LANES), lambda b, ea, eb, nv: (0, 0)),
        wspec(0, (d, de)), wspec(0, (d, de)), wspec(0, (de, d)),
        wspec(1, (d, de)), wspec(1, (d, de)), wspec(1, (de, d)),
    ]
    args = [blk_ea, blk_eb, blk_nv, src3, src3, dst3, dst3, x, g.reshape(1, d), w_rt, b_rt, wg, wu, wd, wg, wu, wd]
    if final:
        in_specs.append(pl.BlockSpec((1, d), lambda b, ea, eb, nv: (0, 0)))
        args.append(g_final.reshape(1, d))
    grid_spec = pltpu.PrefetchScalarGridSpec(
        num_scalar_prefetch=3,
        grid=(nb,),
        in_specs=in_specs,
        out_specs=pl.BlockSpec(memory_space=pl.ANY),
        scratch_shapes=[pltpu.VMEM((2, tb * (d // LANES), LANES), F32), pltpu.VMEM((2, tb, d), F32),
                        pltpu.SemaphoreType.DMA((2,)), pltpu.SemaphoreType.DMA((2,))],
    )
    return pl.pallas_call(
        functools.partial(_moe_kernel, tb=tb, n_tok=t, n_groups=n_groups, final=final),
        grid_spec=grid_spec,
        out_shape=jax.ShapeDtypeStruct((t + 2 * tb, d), F32),
        compiler_params=_params(("arbitrary",), 56),
        name="moe_mlp",
    )(*args)


def _pair_tables(n_groups, per_group):
    ea, eb = [], []
    for grp in range(n_groups):
        for a in range(per_group):
            for b in range(a + 1, per_group):
                ea.append(grp * per_group + a)
                eb.append(grp * per_group + b)
    return jnp.asarray(ea, jnp.int32), jnp.asarray(eb, jnp.int32)


def _hier_moe(x, t, layer, p, tb, g_final=None):
    n_groups, n_experts = p["n_groups"], p["n_experts"]
    per_group = n_experts // n_groups
    n_buckets = n_groups * (per_group * (per_group - 1) // 2)
    assert n_buckets <= LANES
    route, cnt, xrow = _route(x, t, p["g_ffn"][layer], p["w_rt"][layer], p["b_rt"][layer], n_groups, per_group)

    key = route[:, 0].astype(jnp.int32)
    rank = route[:, 1].astype(jnp.int32)
    counts = cnt[0, :n_buckets].astype(jnp.int32)
    padded = (counts + tb - 1) // tb * tb
    pend = jnp.cumsum(padded)
    pstart = pend - padded
    bucket_ids = jnp.arange(n_buckets, dtype=jnp.int32)
    dest = jnp.sum(jnp.where(key[:, None] == bucket_ids[None, :], pstart[None, :], 0), axis=1) + rank
    n_blocks = -(-(t + n_buckets * (tb - 1)) // tb)
    n_slots = n_blocks * tb
    blk_start = jnp.arange(n_blocks, dtype=jnp.int32) * tb
    blk_bucket = jnp.minimum(jnp.sum((pend[None, :] <= blk_start[:, None]).astype(jnp.int32), axis=1), n_buckets - 1)
    blk_nv = jnp.clip(counts[blk_bucket] - (blk_start - pstart[blk_bucket]), 0, tb).astype(jnp.int32)
    tab_a, tab_b = _pair_tables(n_groups, per_group)
    blk_ea, blk_eb = tab_a[blk_bucket], tab_b[blk_bucket]

    slot_tok = jnp.zeros((n_slots,), jnp.int32).at[dest].set(jnp.arange(t, dtype=jnp.int32))
    slot = jnp.arange(n_slots, dtype=jnp.int32)
    in_blk, blk = slot % tb, slot // tb
    valid = in_blk < jnp.repeat(blk_nv, tb)
    slot_dst = jnp.where(valid, slot_tok, t + (blk % 2) * tb + in_blk)

    return _moe_mlp(xrow, t, p["g_ffn"][layer], p["w_rt"][layer], p["b_rt"][layer], n_groups, p["w_gate"], p["w_up"],
                    p["w_down"], layer, blk_ea, blk_eb, blk_nv, slot_tok, slot_dst, tb, g_final)


def _rms_proj_kernel(x_ref, g_ref, w_ref, *out_refs):
    res = jnp.dot(_rms(x_ref[...], g_ref[...]).astype(BF16), w_ref[...], preferred_element_type=F32)
    for o in out_refs:
        o[...] = res.astype(o.dtype)


def _proj_tiles(t, n):
    return _pick(t, (512, 256, 128)), _pick(n, (2048, 1024, 512, 256, 128))


def _rms_proj(x, t, g, w_bf, out_dtypes):
    d = x.shape[1]
    n = w_bf.shape[1]
    tm, tn = _proj_tiles(t, n)
    return pl.pallas_call(
        _rms_proj_kernel,
        grid=(n // tn, t // tm),
        in_specs=[
            pl.BlockSpec((tm, d), lambda j, i: (i, 0)),
            pl.BlockSpec((1, d), lambda j, i: (0, 0)),
            pl.BlockSpec((d, tn), lambda j, i: (0, j)),
        ],
        out_specs=[pl.BlockSpec((tm, tn), lambda j, i: (i, j)) for _ in out_dtypes],
        out_shape=[jax.ShapeDtypeStruct((t, n), dt) for dt in out_dtypes],
        compiler_params=_params(("parallel", "parallel"), 56),
        name="rms_proj",
    )(x, g.reshape(1, d), w_bf)


def _proj_res_kernel(o_ref, w_ref, x_ref, out_ref):
    out_ref[...] = x_ref[...] + jnp.dot(o_ref[...], w_ref[...], preferred_element_type=F32)


def _proj_res(o_bf, w_bf, x, t):
    d = x.shape[1]
    k = o_bf.shape[1]
    tm, tn = _proj_tiles(t, d)
    return pl.pallas_call(
        _proj_res_kernel,
        grid=(d // tn, t // tm),
        in_specs=[
            pl.BlockSpec((tm, k), lambda j, i: (i, 0)),
            pl.BlockSpec((k, tn), lambda j, i: (0, j)),
            pl.BlockSpec((tm, tn), lambda j, i: (i, j)),
        ],
        out_specs=pl.BlockSpec((tm, tn), lambda j, i: (i, j)),
        out_shape=jax.ShapeDtypeStruct((t, d), F32),
        compiler_params=_params(("parallel", "parallel"), 56),
        name="proj_res",
    )(o_bf, w_bf, x)


def _attn_heads(q_ref, kv_refs, bias_ref, o_ref, i, *, n_seg, n_heads, hd, scale, clamp_mask):
    def seg(ref, h):
        sl = slice(h * hd, (h + 1) * hd)
        return ref[0, :, sl] if len(ref.shape) == 3 else ref[:, sl]

    for h in range(n_heads):
        sl = slice(h * hd, (h + 1) * hd)
        qh = q_ref[:, sl]
        scores = []
        off = 0
        for si in range(n_seg):
            kh = seg(kv_refs[2 * si], h)
            nk = kh.shape[0]
            s = lax.dot_general(qh, kh, (((1,), (1,)), ((), ())), preferred_element_type=F32) * scale
            s = s + bias_ref[h, :, off:off + nk]
            if clamp_mask and si < n_seg - 1:
                s = jnp.where(i >= n_seg - 1 - si, s, NEG_INF)
            scores.append(s)
            off += nk
        m = functools.reduce(jnp.maximum, [jnp.max(s, axis=-1, keepdims=True) for s in scores])
        ps = [jnp.exp2(s - m) for s in scores]
        l = functools.reduce(lambda a, b: a + b, [jnp.sum(p, axis=-1, keepdims=True) for p in ps])
        acc = None
        for si in range(n_seg):
            pv = jnp.dot(ps[si].astype(BF16), seg(kv_refs[2 * si + 1], h), preferred_element_type=F32)
            acc = pv if acc is None else acc + pv
        o_ref[:, sl] = (acc / l).astype(o_ref.dtype)


def _attn_prompt_kernel(*refs, nq, n_real, n_seg, **kw):
    q_ref, kv_refs, bias_ref, o_ref = refs[0], refs[1:1 + 2 * n_seg], refs[1 + 2 * n_seg], refs[-1]
    step = pl.program_id(0)

    @pl.when(step >= n_real)
    def _():
        o_ref[...] = jnp.zeros_like(o_ref)

    @pl.when(step < n_real)
    def _():
        _attn_heads(q_ref, kv_refs, bias_ref, o_ref, lax.rem(step, nq), n_seg=n_seg, **kw)

def _band_bias(table, qb, n_keys, left):
    clip = (table.shape[-1] - 1) // 2
    n_heads = table.shape[0]
    w = qb + n_keys
    m = jnp.arange(w, dtype=jnp.int32)
    delta = jnp.where(m < n_keys, m, m - w)
    u = table[:, jnp.clip(left - delta, -clip, clip) + clip].astype(F32)
    toe = jnp.tile(u, (1, qb))[:, :qb * (w - 1)].reshape(n_heads, qb, w - 1)[:, :, :n_keys]
    i = jnp.arange(qb, dtype=jnp.int32)[:, None]
    j = jnp.arange(n_keys, dtype=jnp.int32)[None, :]
    kpos = j - left
    cq = (i // CHUNK) * CHUNK
    inband = (kpos >= cq - LEFT_LEN) & (kpos < cq + CHUNK)
    return jnp.where(inband[None], toe * LOG2E, NEG_INF)


def _attn_prompt(q, kv, table, batch, seq, n_heads, n_tok_total):
    d = q.shape[1]
    hd = d // n_heads
    qb = _pick(seq, (256, 128, 64))
    nq = seq // qb
    n_seg = LEFT_LEN // qb + 1
    n_real = batch * nq
    n_fill_rows = n_tok_total - batch * seq
    assert n_fill_rows % qb == 0
    bias = _band_bias(table, qb, n_seg * qb, LEFT_LEN)

    def kv_spec(si, col):
        back = n_seg - 1 - si

        def index(step):
            r = jnp.minimum(step, n_real - 1)
            return (r - jnp.minimum(r % nq, back), col)
        return pl.BlockSpec((qb, d), index)

    in_specs = [pl.BlockSpec((qb, d), lambda step: (jnp.minimum(step, n_real - 1), 0))]
    args = [q]
    for si in range(n_seg):
        in_specs += [kv_spec(si, 0), kv_spec(si, 1)]
        args += [kv, kv]
    in_specs.append(pl.BlockSpec((n_heads, qb, n_seg * qb), lambda step: (0, 0, 0)))
    args.append(bias)
    return pl.pallas_call(
        functools.partial(_attn_prompt_kernel, nq=nq, n_real=n_real, n_seg=n_seg, n_heads=n_heads, hd=hd,
                          scale=hd ** -0.5 * LOG2E, clamp_mask=True),
        grid=(n_real + n_fill_rows // qb,),
        in_specs=in_specs,
        out_specs=pl.BlockSpec((qb, d), lambda step: (step, 0)),
        out_shape=jax.ShapeDtypeStruct((n_tok_total, d), BF16),
        compiler_params=_params(("arbitrary",), 56),
        name="attn_prompt",
    )(*args)


def _attn_sample_kernel(*refs, n_seg, **kw):
    _attn_heads(refs[0], refs[1:1 + 2 * n_seg], refs[1 + 2 * n_seg], refs[-1], 0, n_seg=n_seg, **kw)


def _attn_sample(q, kv, cache_k, cache_v, table, row0, batch, seq, n_heads, o_buf):
    d = q.shape[1]
    hd = d // n_heads
    past = cache_k.shape[1]
    assert row0 % seq == 0
    rb = row0 // seq
    assert seq <= CHUNK and past == LEFT_LEN
    bias = _band_bias(table, seq, past + seq, past)
    in_specs = [
        pl.BlockSpec((seq, d), lambda b, i: (rb + b, 0)),
        pl.BlockSpec((1, past, d), lambda b, i: (b, 0, 0)),
        pl.BlockSpec((1, past, d), lambda b, i: (b, 0, 0)),
        pl.BlockSpec((seq, d), lambda b, i: (rb + b, 0)),
        pl.BlockSpec((seq, d), lambda b, i: (rb + b, 1)),
        pl.BlockSpec((n_heads, seq, past + seq), lambda b, i: (0, 0, 0)),
        pl.BlockSpec(memory_space=pl.ANY),
    ]
    return pl.pallas_call(
        functools.partial(_attn_sample_kernel, n_seg=2, n_heads=n_heads, hd=hd, scale=hd ** -0.5 * LOG2E,
                          clamp_mask=False),
        grid=(batch, 1),
        in_specs=in_specs,
        out_specs=pl.BlockSpec((seq, d), lambda b, i: (rb + b, 0)),
        out_shape=jax.ShapeDtypeStruct(o_buf.shape, BF16),
        input_output_aliases={6: 0},
        compiler_params=_params(("parallel", "arbitrary"), 56),
        name="attn_sample",
    )(q, cache_k, cache_v, kv, kv, bias, o_buf)


def kernel(x_prompt, x_sample, state_pool, cache_k, cache_v, g_pool, w_pool, s_pool, g_kv, w_kv, g_attn, w_q, w_o, rel_bias, g_ffn, w_group, b_group, w_router, b_router, w_gate, w_up, w_down, g_final):
    bp, lp, d = x_prompt.shape
    bs, ls, _ = x_sample.shape
    depth = g_ffn.shape[0]
    n_pool = g_pool.shape[0]
    n_heads, hd = cache_k.shape[2], cache_k.shape[3]
    n_groups, n_experts = w_group.shape[-1], w_router.shape[-1]
    tp, tsm = bp * lp, bs * ls
    t = tp + tsm
    tb = 128

    w_rt = jnp.concatenate([w_group, w_router], axis=-1)
    w_rt = jnp.pad(w_rt, ((0, 0), (0, 0), (0, LANES - w_rt.shape[-1]))).astype(BF16)
    b_rt = jnp.concatenate([b_group, b_router], axis=-1)
    b_rt = jnp.pad(b_rt, ((0, 0), (0, LANES - b_rt.shape[-1])))[:, None, :]
    moe_p = dict(n_groups=n_groups, n_experts=n_experts, g_ffn=g_ffn, w_rt=w_rt, b_rt=b_rt,
                 w_gate=w_gate.astype(BF16), w_up=w_up.astype(BF16), w_down=w_down.astype(BF16))
    w_pool_bf = w_pool.astype(BF16)
    w_kv_bf = w_kv.astype(BF16)
    w_q_bf = w_q.astype(BF16)
    w_o_bf = w_o.astype(BF16)

    zero_halo = jnp.zeros((1, HALO, d), F32)
    samp_halo = jnp.pad(state_pool, ((0, 0), (0, 0), (1, 0), (0, 0)))

    ck = cache_k.reshape(bs, cache_k.shape[1], d).astype(BF16)
    cv = cache_v.reshape(bs, cache_v.shape[1], d).astype(BF16)

    x = None
    hist_p, hist_s = [], []
    kv_f32 = kv_bf = None
    for layer in range(depth):
        if layer < n_pool:
            if layer == 0:
                src_p, off_p, src_s, off_s = x_prompt.reshape(tp, d), 0, x_sample.reshape(tsm, d), 0
            else:
                src_p, off_p, src_s, off_s = x, 0, x, tp
            buf, hp = _pool_layer(src_p, off_p, None, 0, t, bp, lp, zero_halo, g_pool[layer], w_pool_bf[layer],
                                  s_pool[layer], 0, n_fill_rows=tsm)
            x, hs = _pool_layer(src_s, off_s, buf, tp, t, bs, ls, samp_halo[layer], g_pool[layer], w_pool_bf[layer],
                                s_pool[layer], PAST_LEN)
            hist_p.append(hp)
            hist_s.append(hs)
        else:
            j = layer - n_pool
            if j == 0:
                kv_f32, kv_bf = _rms_proj(x, t, g_kv, w_kv_bf, (F32, BF16))
            (q,) = _rms_proj(x, t, g_attn[j], w_q_bf[j], (BF16,))
            o = _attn_prompt(q, kv_bf, rel_bias[j], bp, lp, n_heads, t)
            o = _attn_sample(q, kv_bf, ck, cv, rel_bias[j], tp, bs, ls, n_heads, o)
            x = _proj_res(o, w_o_bf[j], x, t)
        x = _hier_moe(x, t, layer, moe_p, tb, g_final if layer == depth - 1 else None)

    y_prompt = x[:tp].reshape(bp, lp, d)
    y_sample = x[tp:t].reshape(bs, ls, d)
    keep = min(LEFT_LEN, lp)
    if tsm % lp == 0:
        kvp = kv_f32.reshape(t // lp, lp, 2 * d)[:bp, lp - keep:]
    else:
        kvp = kv_f32[:tp].reshape(bp, lp, 2 * d)[:, lp - keep:]
    kvs = kv_f32[tp:].reshape(bs, ls, 2 * d)
    return (y_prompt, y_sample, jnp.stack(hist_p, axis=0), jnp.stack(hist_s, axis=0),
            kvp[..., :d].reshape(bp, keep, n_heads, hd), kvp[..., d:].reshape(bp, keep, n_heads, hd),
            kvs[..., :d].reshape(bs, ls, n_heads, hd), kvs[..., d:].reshape(bs, ls, n_heads, hd))
```

```python
import functools

import jax
import jax.numpy as jnp
from jax import lax
from jax.experimental import pallas as pl
from jax.experimental.pallas import tpu as pltpu

RMS_EPS = 1e-6
NEG_INF = -1e30
CHUNK = 64
LEFT_CHUNKS = 8
LEFT_LEN = LEFT_CHUNKS * CHUNK
POOL_WINDOWS = (2, 4, 8, 16)
POOL_HIST = max(POOL_WINDOWS) - 1
HALO = POOL_HIST + 1
PAST_LEN = 1024
LANES = 128
LOG2E = 1.4426950408889634

F32 = jnp.float32
BF16 = jnp.bfloat16


def _pick(n, cands):
    for c in cands:
        if n % c == 0:
            return c
    raise ValueError(f"no tile for {n} in {cands}")


def _params(sem, vmem_mib=None):
    return pltpu.CompilerParams(
        dimension_semantics=sem,
        vmem_limit_bytes=None if vmem_mib is None else vmem_mib * 1024 * 1024)


def _rms(x, g):
    ms = jnp.mean(x * x, axis=-1, keepdims=True)
    return (x * lax.rsqrt(ms + RMS_EPS)) * g


def _pool_kernel(*refs, ts, ns, n_real, dg, pos0, aliased):
    if aliased:
        x_ref, halo0_ref, g_ref, w_ref, s_ref, _, out_ref, hist_ref, halo_sc = refs
    else:
        x_ref, halo0_ref, g_ref, w_ref, s_ref, out_ref, hist_ref, halo_sc = refs
    step = pl.program_id(0)
    s = lax.rem(step, ns)

    @pl.when(step >= n_real)
    def _():
        out_ref[...] = jnp.zeros_like(out_ref)

    @pl.when(step < n_real)
    def _():
        @pl.when(s == 0)
        def _():
            halo_sc[...] = halo0_ref[0]

        x = x_ref[...]
        h = _rms(x, g_ref[...])
        halo = halo_sc[...]
        pos = lax.broadcasted_iota(jnp.int32, (ts, 1), 0) + (s * ts + pos0)
        for gi, w in enumerate(POOL_WINDOWS):
            lo, hi = gi * dg, (gi + 1) * dg
            hg = h[:, lo:hi]
            acc = jnp.concatenate([halo[:, lo:hi], hg], axis=0)
            k = 1
            while k < w:
                acc = acc + pltpu.roll(acc, k, 0)
                k *= 2
            cnt = jnp.minimum(pos + 1, w).astype(F32)
            pooled = acc[HALO:, :] * (1.0 / cnt) - hg
            mix = jnp.dot(pooled.astype(BF16), w_ref[gi], preferred_element_type=F32) * s_ref[:, lo:hi]
            out_ref[:, lo:hi] = x[:, lo:hi] + mix
        halo_sc[...] = h[ts - HALO:, :]

        @pl.when(s == ns - 1)
        def _():
            hist_ref[0] = halo_sc[1:HALO, :]


def _pool_layer(x_in, in_off, out_buf, out_off, n_tok_total, batch, seq, halo0, g, w_bf, s, pos0, n_fill_rows=0):
    d = x_in.shape[1]
    dg = d // len(POOL_WINDOWS)
    ts = _pick(seq, (256, 128, 64))
    ns = seq // ts
    assert in_off % ts == 0 and out_off % ts == 0 and n_fill_rows % ts == 0
    ib, ob = in_off // ts, out_off // ts
    n_real = batch * ns
    aliased = out_buf is not None
    halo_b = halo0.shape[0]

    def row(i):
        return jnp.minimum(i, n_real - 1) // ns

    in_specs = [
        pl.BlockSpec((ts, d), lambda i: (ib + jnp.minimum(i, n_real - 1), 0)),
        pl.BlockSpec((1, HALO, d), (lambda i: (row(i), 0, 0)) if halo_b > 1 else (lambda i: (0, 0, 0))),
        pl.BlockSpec((1, d), lambda i: (0, 0)),
        pl.BlockSpec((len(POOL_WINDOWS), dg, dg), lambda i: (0, 0, 0)),
        pl.BlockSpec((1, d), lambda i: (0, 0)),
    ]
    args = [x_in, halo0, g.reshape(1, d), w_bf, s.reshape(1, d)]
    if aliased:
        in_specs.append(pl.BlockSpec(memory_space=pl.ANY))
        args.append(out_buf)
    return pl.pallas_call(
        functools.partial(_pool_kernel, ts=ts, ns=ns, n_real=n_real, dg=dg, pos0=pos0, aliased=aliased),
        grid=(n_real + n_fill_rows // ts,),
        in_specs=in_specs,
        out_specs=[
            pl.BlockSpec((ts, d), lambda i: (ob + i, 0)),
            pl.BlockSpec((1, POOL_HIST, d), lambda i: (row(i), 0, 0)),
        ],
        out_shape=[
            jax.ShapeDtypeStruct((n_tok_total, d), F32),
            jax.ShapeDtypeStruct((batch, POOL_HIST, d), F32),
        ],
        scratch_shapes=[pltpu.VMEM((HALO, d), F32)],
        input_output_aliases={len(args) - 1: 0} if aliased else {},
        compiler_params=_params(("arbitrary",), 48),
        name="pool_layer",
    )(*args)


def _router_kernel(x_ref, g_ref, w_ref, b_ref, route_ref, cnt_ref, xrow_ref, carry_sc, *, tm, n_groups, per_group):
    i = pl.program_id(0)

    @pl.when(i == 0)
    def _():
        carry_sc[...] = jnp.zeros_like(carry_sc)

    x = x_ref[...]
    nc = x.shape[1] // LANES
    for c in range(nc):
        xrow_ref[pl.ds(c, tm, stride=nc), :] = x[:, c * LANES:(c + 1) * LANES]

    h = _rms(x, g_ref[...])
    lg = jnp.dot(h.astype(BF16), w_ref[...], preferred_element_type=F32) + b_ref[...]
    lane = lax.broadcasted_iota(jnp.int32, (tm, LANES), 1).astype(F32)
    big = float(LANES)

    def first_max(vals, mask):
        v = jnp.max(jnp.where(mask, vals, -jnp.inf), axis=-1, keepdims=True)
        idx = jnp.min(jnp.where(mask & (vals == v), lane, big), axis=-1, keepdims=True)
        return v, idx

    gmask = lane < n_groups
    _, gsel = first_max(lg, gmask)
    lo = n_groups + gsel * per_group
    emask = (lane >= lo) & (lane < lo + per_group)
    _, i1 = first_max(lg, emask)
    _, i2 = first_max(lg, emask & (lane != i1))

    a = jnp.minimum(i1, i2) - lo
    b = jnp.maximum(i1, i2) - lo
    n_pairs = per_group * (per_group - 1) // 2
    key = gsel * n_pairs + a * (2 * per_group - 1 - a) * 0.5 + (b - a - 1.0)
    sel = lane == key
    onehot = jnp.where(sel, 1.0, 0.0)
    r_i = lax.broadcasted_iota(jnp.int32, (tm, tm), 0)
    c_i = lax.broadcasted_iota(jnp.int32, (tm, tm), 1)
    lower = jnp.where(c_i < r_i, 1.0, 0.0).astype(BF16)
    before = jnp.dot(lower, onehot.astype(BF16), preferred_element_type=F32) + carry_sc[...]
    rank = jnp.sum(jnp.where(sel, before, 0.0), axis=-1, keepdims=True)
    carry = carry_sc[...] + jnp.sum(onehot, axis=0, keepdims=True)
    carry_sc[...] = carry
    cnt_ref[...] = carry

    route_ref[...] = jnp.where(lane == 0, key, jnp.where(lane == 1, rank, 0.0))


def _route(x, t, g, w_rt, b_rt, n_groups, per_group):
    d = x.shape[1]
    tm = _pick(t, (512, 256, 128))
    return pl.pallas_call(
        functools.partial(_router_kernel, tm=tm, n_groups=n_groups, per_group=per_group),
        grid=(t // tm,),
        in_specs=[
            pl.BlockSpec((tm, d), lambda i: (i, 0)),
            pl.BlockSpec((1, d), lambda i: (0, 0)),
            pl.BlockSpec((d, LANES), lambda i: (0, 0)),
            pl.BlockSpec((1, LANES), lambda i: (0, 0)),
        ],
        out_specs=[
            pl.BlockSpec((tm, LANES), lambda i: (i, 0)),
            pl.BlockSpec((1, LANES), lambda i: (0, 0)),
            pl.BlockSpec((tm * (d // LANES), LANES), lambda i: (i, 0)),
        ],
        out_shape=[
            jax.ShapeDtypeStruct((t, LANES), F32),
            jax.ShapeDtypeStruct((1, LANES), F32),
            jax.ShapeDtypeStruct((t * (d // LANES), LANES), F32),
        ],
        scratch_shapes=[pltpu.VMEM((1, LANES), F32)],
        compiler_params=_params(("arbitrary",), 48),
        name="moe_route",
    )(x, g.reshape(1, d), w_rt, b_rt)


def _moe_kernel(ea_ref, eb_ref, nv_ref, srcc_ref, srcn_ref, dst_ref, x_hbm, g_ref, wrt_ref, brt_ref,
                wga_ref, wua_ref, wda_ref, wgb_ref, wub_ref, wdb_ref, *rest, tb, n_tok, n_groups, final):
    if final:
        gf_ref, out_hbm, xbuf, obuf, gsem, ssem = rest
    else:
        out_hbm, xbuf, obuf, gsem, ssem = rest
    b = pl.program_id(0)
    nb = pl.num_programs(0)
    slot = lax.rem(b, 2)
    nv = nv_ref[b]
    nv_next = nv_ref[jnp.minimum(b + 1, nb - 1)]
    nv_prev2 = nv_ref[jnp.maximum(b - 2, 0)]
    nv_prev1 = nv_ref[jnp.maximum(b - 1, 0)]

    nc = xbuf.shape[1] // tb

    def gather(idx_ref, s, lo=0, hi=tb):
        for r in range(lo, hi):
            src = x_hbm.at[pl.ds(pl.multiple_of(idx_ref[0, 0, r] * nc, nc), nc), :]
            pltpu.make_async_copy(src, xbuf.at[s, pl.ds(r * nc, nc), :], gsem.at[s]).start()

    def gather_wait(s):
        pltpu.make_async_copy(x_hbm.at[pl.ds(0, tb * nc), :], xbuf.at[s], gsem.at[s]).wait()

    def scatter_wait(s):
        pltpu.make_async_copy(obuf.at[s], out_hbm.at[pl.ds(0, tb), :], ssem.at[s]).wait()

    @pl.when(b == 0)
    def _():
        obuf[...] = jnp.zeros_like(obuf)
        for s in range(2):
            spare = pltpu.make_async_copy(obuf.at[s], out_hbm.at[pl.ds(n_tok + s * tb, tb), :], ssem.at[s])
            spare.start()
            spare.wait()

    def scatter(idx_ref, s):
        for r in range(tb):
            pltpu.make_async_copy(obuf.at[s, pl.ds(r, 1), :], out_hbm.at[pl.ds(idx_ref[0, 0, r], 1), :],
                                  ssem.at[s]).start()

    n_pieces = 8
    piece = tb // n_pieces
    prefetch_next = (b + 1 < nb) & (nv_next > 0)

    def gather_piece(k):
        @pl.when(prefetch_next)
        def _():
            gather(srcn_ref, 1 - slot, k * piece, (k + 1) * piece)

    @pl.when((b == 0) & (nv > 0))
    def _():
        gather(srcc_ref, 0)

    @pl.when((b >= 2) & (nv_prev2 > 0))
    def _():
        scatter_wait(slot)

    @pl.when(nv > 0)
    def _():
        gather_wait(slot)
        gather_piece(0)
        xrows = xbuf.at[slot]
        x = jnp.concatenate([xrows[pl.ds(c, tb, stride=nc), :] for c in range(nc)], axis=1)
        hn = _rms(x, g_ref[...]).astype(BF16)
        gather_piece(1)
        stage = [2]

        def expert(wg_ref, wu_ref, wd_ref):
            a = jnp.dot(hn, wg_ref[...], preferred_element_type=F32)
            gather_piece(stage[0])
            u = jnp.dot(hn, wu_ref[...], preferred_element_type=F32)
            gather_piece(stage[0] + 1)
            act = (a * (1.0 / (1.0 + jnp.exp(-a)))) * u
            y = jnp.dot(act.astype(BF16), wd_ref[...], preferred_element_type=F32)
            gather_piece(stage[0] + 2)
            stage[0] += 3
            return y

        lg = jnp.dot(hn, wrt_ref[...], preferred_element_type=F32) + brt_ref[...]
        lane = lax.broadcasted_iota(jnp.int32, (tb, LANES), 1)
        gmask = lane < n_groups
        gmax = jnp.max(jnp.where(gmask, lg, -jnp.inf), axis=-1, keepdims=True)
        g_prob = 1.0 / jnp.sum(jnp.where(gmask, jnp.exp(lg - gmax), 0.0), axis=-1, keepdims=True)
        la = jnp.sum(jnp.where(lane == n_groups + ea_ref[b], lg, 0.0), axis=-1, keepdims=True)
        lb = jnp.sum(jnp.where(lane == n_groups + eb_ref[b], lg, 0.0), axis=-1, keepdims=True)
        top = jnp.maximum(la, lb)
        pa = jnp.exp(la - top)
        pb = jnp.exp(lb - top)
        den = pa + pb
        gate_a = (pa / den) * g_prob
        gate_b = (pb / den) * g_prob

        ya = expert(wga_ref, wua_ref, wda_ref)
        yb = expert(wgb_ref, wub_ref, wdb_ref)
        out = x + (gate_a * ya + gate_b * yb)
        if final:
            out = _rms(out, gf_ref[...])
        obuf[slot] = out
        scatter(dst_ref, slot)

    @pl.when(b == nb - 1)
    def _():
        @pl.when((b >= 1) & (nv_prev1 > 0))
        def _():
            scatter_wait(1 - slot)

        @pl.when(nv > 0)
        def _():
            scatter_wait(slot)


def _moe_mlp(xrow, t, g, w_rt, b_rt, n_groups, wg, wu, wd, layer, blk_ea, blk_eb, blk_nv, slot_src, slot_dst, tb,
             g_final):
    x = xrow
    d = wg.shape[-2]
    de = wg.shape[-1]
    nb = blk_nv.shape[0]
    src3 = slot_src.reshape(nb, 1, tb)
    dst3 = slot_dst.reshape(nb, 1, tb)
    final = g_final is not None

    def wspec(which, shape):
        return pl.BlockSpec((None, None) + shape, lambda b, ea, eb, nv: (layer, (ea, eb)[which][b], 0, 0))

    in_specs = [
        pl.BlockSpec((1, 1, tb), lambda b, ea, eb, nv: (b, 0, 0), memory_space=pltpu.SMEM),
        pl.BlockSpec((1, 1, tb), lambda b, ea, eb, nv: (jnp.minimum(b + 1, nb - 1), 0, 0), memory_space=pltpu.SMEM),
        pl.BlockSpec((1, 1, tb), lambda b, ea, eb, nv: (b, 0, 0), memory_space=pltpu.SMEM),
        pl.BlockSpec(memory_space=pl.ANY),
        pl.BlockSpec((1, d), lambda b, ea, eb, nv: (0, 0)),
        pl.BlockSpec((d, LANES), lambda b, ea, eb, nv: (0, 0)),
        pl.BlockSpec((1, LANES), lambda b, ea, eb, nv: (0, 0)),
        wspec(0, (d, de)), wspec(0, (d, de)), wspec(0, (de, d)),
        wspec(1, (d, de)), wspec(1, (d, de)), wspec(1, (de, d)),
    ]
    args = [blk_ea, blk_eb, blk_nv, src3, src3, dst3, x, g.reshape(1, d), w_rt, b_rt, wg, wu, wd, wg, wu, wd]
    if final:
        in_specs.append(pl.BlockSpec((1, d), lambda b, ea, eb, nv: (0, 0)))
        args.append(g_final.reshape(1, d))
    grid_spec = pltpu.PrefetchScalarGridSpec(
        num_scalar_prefetch=3,
        grid=(nb,),
        in_specs=in_specs,
        out_specs=pl.BlockSpec(memory_space=pl.ANY),
        scratch_shapes=[pltpu.VMEM((2, tb * (d // LANES), LANES), F32), pltpu.VMEM((2, tb, d), F32),
                        pltpu.SemaphoreType.DMA((2,)), pltpu.SemaphoreType.DMA((2,))],
    )
    return pl.pallas_call(
        functools.partial(_moe_kernel, tb=tb, n_tok=t, n_groups=n_groups, final=final),
        grid_spec=grid_spec,
        out_shape=jax.ShapeDtypeStruct((t + 2 * tb, d), F32),
        compiler_params=_params(("arbitrary",), 56),
        name="moe_mlp",
    )(*args)


def _pair_tables(n_groups, per_group):
    ea, eb = [], []
    for grp in range(n_groups):
        for a in range(per_group):
            for b in range(a + 1, per_group):
                ea.append(grp * per_group + a)
                eb.append(grp * per_group + b)
    return jnp.asarray(ea, jnp.int32), jnp.asarray(eb, jnp.int32)


def _hier_moe(x, t, layer, p, tb, g_final=None):
    n_groups, n_experts = p["n_groups"], p["n_experts"]
    per_group = n_experts // n_groups
    n_buckets = n_groups * (per_group * (per_group - 1) // 2)
    assert n_buckets <= LANES
    route, cnt, xrow = _route(x, t, p["g_ffn"][layer], p["w_rt"][layer], p["b_rt"][layer], n_groups, per_group)

    key = route[:, 0].astype(jnp.int32)
    rank = route[:, 1].astype(jnp.int32)
    counts = cnt[0, :n_buckets].astype(jnp.int32)
    padded = (counts + tb - 1) // tb * tb
    pend = jnp.cumsum(padded)
    pstart = pend - padded
    bucket_ids = jnp.arange(n_buckets, dtype=jnp.int32)
    dest = jnp.sum(jnp.where(key[:, None] == bucket_ids[None, :], pstart[None, :], 0), axis=1) + rank
    n_blocks = -(-(t + n_buckets * (tb - 1)) // tb)
    n_slots = n_blocks * tb
    blk_start = jnp.arange(n_blocks, dtype=jnp.int32) * tb
    blk_bucket = jnp.minimum(jnp.sum((pend[None, :] <= blk_start[:, None]).astype(jnp.int32), axis=1), n_buckets - 1)
    blk_nv = jnp.clip(counts[blk_bucket] - (blk_start - pstart[blk_bucket]), 0, tb).astype(jnp.int32)
    tab_a, tab_b = _pair_tables(n_groups, per_group)
    blk_ea, blk_eb = tab_a[blk_bucket], tab_b[blk_bucket]

    slot_tok = jnp.zeros((n_slots,), jnp.int32).at[dest].set(jnp.arange(t, dtype=jnp.int32))
    slot = jnp.arange(n_slots, dtype=jnp.int32)
    in_blk, blk = slot % tb, slot // tb
    valid = in_blk < jnp.repeat(blk_nv, tb)
    slot_dst = jnp.where(valid, slot_tok, t + (blk % 2) * tb + in_blk)

    return _moe_mlp(xrow, t, p["g_ffn"][layer], p["w_rt"][layer], p["b_rt"][layer], n_groups, p["w_gate"], p["w_up"],
                    p["w_down"], layer, blk_ea, blk_eb, blk_nv, slot_tok, slot_dst, tb, g_final)


def _rms_proj_kernel(x_ref, g_ref, w_ref, *out_refs):
    res = jnp.dot(_rms(x_ref[...], g_ref[...]).astype(BF16), w_ref[...], preferred_element_type=F32)
    for o in out_refs:
        o[...] = res.astype(o.dtype)


def _proj_tiles(t, n):
    return _pick(t, (512, 256, 128)), _pick(n, (2048, 1024, 512, 256, 128))


def _rms_proj(x, t, g, w_bf, out_dtypes):
    d = x.shape[1]
    n = w_bf.shape[1]
    tm, tn = _proj_tiles(t, n)
    return pl.pallas_call(
        _rms_proj_kernel,
        grid=(n // tn, t // tm),
        in_specs=[
            pl.BlockSpec((tm, d), lambda j, i: (i, 0)),
            pl.BlockSpec((1, d), lambda j, i: (0, 0)),
            pl.BlockSpec((d, tn), lambda j, i: (0, j)),
        ],
        out_specs=[pl.BlockSpec((tm, tn), lambda j, i: (i, j)) for _ in out_dtypes],
        out_shape=[jax.ShapeDtypeStruct((t, n), dt) for dt in out_dtypes],
        compiler_params=_params(("parallel", "parallel"), 56),
        name="rms_proj",
    )(x, g.reshape(1, d), w_bf)


def _proj_res_kernel(o_ref, w_ref, x_ref, out_ref):
    out_ref[...] = x_ref[...] + jnp.dot(o_ref[...], w_ref[...], preferred_element_type=F32)


def _proj_res(o_bf, w_bf, x, t):
    d = x.shape[1]
    k = o_bf.shape[1]
    tm, tn = _proj_tiles(t, d)
    return pl.pallas_call(
        _proj_res_kernel,
        grid=(d // tn, t // tm),
        in_specs=[
            pl.BlockSpec((tm, k), lambda j, i: (i, 0)),
            pl.BlockSpec((k, tn), lambda j, i: (0, j)),
            pl.BlockSpec((tm, tn), lambda j, i: (i, j)),
        ],
        out_specs=pl.BlockSpec((tm, tn), lambda j, i: (i, j)),
        out_shape=jax.ShapeDtypeStruct((t, d), F32),
        compiler_params=_params(("parallel", "parallel"), 56),
        name="proj_res",
    )(o_bf, w_bf, x)


def _attn_heads(q_ref, kv_refs, bias_ref, o_ref, i, *, n_seg, n_heads, hd, scale, clamp_mask):
    def seg(ref, h):
        sl = slice(h * hd, (h + 1) * hd)
        return ref[0, :, sl] if len(ref.shape) == 3 else ref[:, sl]

    for h in range(n_heads):
        sl = slice(h * hd, (h + 1) * hd)
        qh = q_ref[:, sl]
        scores = []
        off = 0
        for si in range(n_seg):
            kh = seg(kv_refs[2 * si], h)
            nk = kh.shape[0]
            s = lax.dot_general(qh, kh, (((1,), (1,)), ((), ())), preferred_element_type=F32) * scale
            s = s + bias_ref[h, :, off:off + nk]
            if clamp_mask and si < n_seg - 1:
                s = jnp.where(i >= n_seg - 1 - si, s, NEG_INF)
            scores.append(s)
            off += nk
        m = functools.reduce(jnp.maximum, [jnp.max(s, axis=-1, keepdims=True) for s in scores])
        ps = [jnp.exp2(s - m) for s in scores]
        l = functools.reduce(lambda a, b: a + b, [jnp.sum(p, axis=-1, keepdims=True) for p in ps])
        acc = None
        for si in range(n_seg):
            pv = jnp.dot(ps[si].astype(BF16), seg(kv_refs[2 * si + 1], h), preferred_element_type=F32)
            acc = pv if acc is None else acc + pv
        o_ref[:, sl] = (acc / l).astype(o_ref.dtype)


def _attn_prompt_kernel(*refs, nq, n_real, n_seg, **kw):
    q_ref, kv_refs, bias_ref, o_ref = refs[0], refs[1:1 + 2 * n_seg], refs[1 + 2 * n_seg], refs[-1]
    step = pl.program_id(0)

    @pl.when(step >= n_real)
    def _():
        o_ref[...] = jnp.zeros_like(o_ref)

    @pl.when(step < n_real)
    def _():
        _attn_heads(q_ref, kv_refs, bias_ref, o_ref, lax.rem(step, nq), n_seg=n_seg, **kw)

def _band_bias(table, qb, n_keys, left):
    clip = (table.shape[-1] - 1) // 2
    n_heads = table.shape[0]
    w = qb + n_keys
    m = jnp.arange(w, dtype=jnp.int32)
    delta = jnp.where(m < n_keys, m, m - w)
    u = table[:, jnp.clip(left - delta, -clip, clip) + clip].astype(F32)
    toe = jnp.tile(u, (1, qb))[:, :qb * (w - 1)].reshape(n_heads, qb, w - 1)[:, :, :n_keys]
    i = jnp.arange(qb, dtype=jnp.int32)[:, None]
    j = jnp.arange(n_keys, dtype=jnp.int32)[None, :]
    kpos = j - left
    cq = (i // CHUNK) * CHUNK
    inband = (kpos >= cq - LEFT_LEN) & (kpos < cq + CHUNK)
    return jnp.where(inband[None], toe * LOG2E, NEG_INF)


def _attn_prompt(q, kv, table, batch, seq, n_heads, n_tok_total):
    d = q.shape[1]
    hd = d // n_heads
    qb = _pick(seq, (256, 128, 64))
    nq = seq // qb
    n_seg = LEFT_LEN // qb + 1
    n_real = batch * nq
    n_fill_rows = n_tok_total - batch * seq
    assert n_fill_rows % qb == 0
    bias = _band_bias(table, qb, n_seg * qb, LEFT_LEN)

    def kv_spec(si, col):
        back = n_seg - 1 - si

        def index(step):
            r = jnp.minimum(step, n_real - 1)
            return (r - jnp.minimum(r % nq, back), col)
        return pl.BlockSpec((qb, d), index)

    in_specs = [pl.BlockSpec((qb, d), lambda step: (jnp.minimum(step, n_real - 1), 0))]
    args = [q]
    for si in range(n_seg):
        in_specs += [kv_spec(si, 0), kv_spec(si, 1)]
        args += [kv, kv]
    in_specs.append(pl.BlockSpec((n_heads, qb, n_seg * qb), lambda step: (0, 0, 0)))
    args.append(bias)
    return pl.pallas_call(
        functools.partial(_attn_prompt_kernel, nq=nq, n_real=n_real, n_seg=n_seg, n_heads=n_heads, hd=hd,
                          scale=hd ** -0.5 * LOG2E, clamp_mask=True),
        grid=(n_real + n_fill_rows // qb,),
        in_specs=in_specs,
        out_specs=pl.BlockSpec((qb, d), lambda step: (step, 0)),
        out_shape=jax.ShapeDtypeStruct((n_tok_total, d), BF16),
        compiler_params=_params(("arbitrary",), 56),
        name="attn_prompt",
    )(*args)


def _attn_sample_kernel(*refs, n_seg, **kw):
    _attn_heads(refs[0], refs[1:1 + 2 * n_seg], refs[1 + 2 * n_seg], refs[-1], 0, n_seg=n_seg, **kw)


def _attn_sample(q, kv, cache_k, cache_v, table, row0, batch, seq, n_heads, o_buf):
    d = q.shape[1]
    hd = d // n_heads
    past = cache_k.shape[1]
    assert row0 % seq == 0
    rb = row0 // seq
    assert seq <= CHUNK and past == LEFT_LEN
    bias = _band_bias(table, seq, past + seq, past)
    in_specs = [
        pl.BlockSpec((seq, d), lambda b, i: (rb + b, 0)),
        pl.BlockSpec((1, past, d), lambda b, i: (b, 0, 0)),
        pl.BlockSpec((1, past, d), lambda b, i: (b, 0, 0)),
        pl.BlockSpec((seq, d), lambda b, i: (rb + b, 0)),
        pl.BlockSpec((seq, d), lambda b, i: (rb + b, 1)),
        pl.BlockSpec((n_heads, seq, past + seq), lambda b, i: (0, 0, 0)),
        pl.BlockSpec(memory_space=pl.ANY),
    ]
    return pl.pallas_call(
        functools.partial(_attn_sample_kernel, n_seg=2, n_heads=n_heads, hd=hd, scale=hd ** -0.5 * LOG2E,
                          clamp_mask=False),
        grid=(batch, 1),
        in_specs=in_specs,
        out_specs=pl.BlockSpec((seq, d), lambda b, i: (rb + b, 0)),
        out_shape=jax.ShapeDtypeStruct(o_buf.shape, BF16),
        input_output_aliases={6: 0},
        compiler_params=_params(("parallel", "arbitrary"), 56),
        name="attn_sample",
    )(q, cache_k, cache_v, kv, kv, bias, o_buf)


def kernel(x_prompt, x_sample, state_pool, cache_k, cache_v, g_pool, w_pool, s_pool, g_kv, w_kv, g_attn, w_q, w_o, rel_bias, g_ffn, w_group, b_group, w_router, b_router, w_gate, w_up, w_down, g_final):
    bp, lp, d = x_prompt.shape
    bs, ls, _ = x_sample.shape
    depth = g_ffn.shape[0]
    n_pool = g_pool.shape[0]
    n_heads, hd = cache_k.shape[2], cache_k.shape[3]
    n_groups, n_experts = w_group.shape[-1], w_router.shape[-1]
    tp, tsm = bp * lp, bs * ls
    t = tp + tsm
    tb = 128

    w_rt = jnp.concatenate([w_group, w_router], axis=-1)
    w_rt = jnp.pad(w_rt, ((0, 0), (0, 0), (0, LANES - w_rt.shape[-1]))).astype(BF16)
    b_rt = jnp.concatenate([b_group, b_router], axis=-1)
    b_rt = jnp.pad(b_rt, ((0, 0), (0, LANES - b_rt.shape[-1])))[:, None, :]
    moe_p = dict(n_groups=n_groups, n_experts=n_experts, g_ffn=g_ffn, w_rt=w_rt, b_rt=b_rt,
                 w_gate=w_gate.astype(BF16), w_up=w_up.astype(BF16), w_down=w_down.astype(BF16))
    w_pool_bf = w_pool.astype(BF16)
    w_kv_bf = w_kv.astype(BF16)
    w_q_bf = w_q.astype(BF16)
    w_o_bf = w_o.astype(BF16)

    zero_halo = jnp.zeros((1, HALO, d), F32)
    samp_halo = jnp.pad(state_pool, ((0, 0), (0, 0), (1, 0), (0, 0)))

    ck = cache_k.reshape(bs, cache_k.shape[1], d).astype(BF16)
    cv = cache_v.reshape(bs, cache_v.shape[1], d).astype(BF16)

    x = None
    hist_p, hist_s = [], []
    kv_f32 = kv_bf = None
    for layer in range(depth):
        if layer < n_pool:
            if layer == 0:
                src_p, off_p, src_s, off_s = x_prompt.reshape(tp, d), 0, x_sample.reshape(tsm, d), 0
            else:
                src_p, off_p, src_s, off_s = x, 0, x, tp
            buf, hp = _pool_layer(src_p, off_p, None, 0, t, bp, lp, zero_halo, g_pool[layer], w_pool_bf[layer],
                                  s_pool[layer], 0, n_fill_rows=tsm)
            x, hs = _pool_layer(src_s, off_s, buf, tp, t, bs, ls, samp_halo[layer], g_pool[layer], w_pool_bf[layer],
                                s_pool[layer], PAST_LEN)
            hist_p.append(hp)
            hist_s.append(hs)
        else:
            j = layer - n_pool
            if j == 0:
                kv_f32, kv_bf = _rms_proj(x, t, g_kv, w_kv_bf, (F32, BF16))
            (q,) = _rms_proj(x, t, g_attn[j], w_q_bf[j], (BF16,))
            o = _attn_prompt(q, kv_bf, rel_bias[j], bp, lp, n_heads, t)
            o = _attn_sample(q, kv_bf, ck, cv, rel_bias[j], tp, bs, ls, n_heads, o)
            x = _proj_res(o, w_o_bf[j], x, t)
        x = _hier_moe(x, t, layer, moe_p, tb, g_final if layer == depth - 1 else None)

    y_prompt = x[:tp].reshape(bp, lp, d)
    y_sample = x[tp:t].reshape(bs, ls, d)
    keep = min(LEFT_LEN, lp)
    if tsm % lp == 0:
        kvp = kv_f32.reshape(t // lp, lp, 2 * d)[:bp, lp - keep:]
    else:
        kvp = kv_f32[:tp].reshape(bp, lp, 2 * d)[:, lp - keep:]
    kvs = kv_f32[tp:].reshape(bs, ls, 2 * d)
    return (y_prompt, y_sample, jnp.stack(hist_p, axis=0), jnp.stack(hist_s, axis=0),
            kvp[..., :d].reshape(bp, keep, n_heads, hd), kvp[..., d:].reshape(bp, keep, n_heads, hd),
            kvs[..., :d].reshape(bs, ls, n_heads, hd), kvs[..., d:].reshape(bs, ls, n_heads, hd))
```

```python
import functools

import jax
import jax.numpy as jnp
from jax import lax
from jax.experimental import pallas as pl
from jax.experimental.pallas import tpu as pltpu

RMS_EPS = 1e-6
NEG_INF = -1e30
CHUNK = 64
LEFT_CHUNKS = 8
LEFT_LEN = LEFT_CHUNKS * CHUNK
POOL_WINDOWS = (2, 4, 8, 16)
POOL_HIST = max(POOL_WINDOWS) - 1
HALO = POOL_HIST + 1
PAST_LEN = 1024
LANES = 128
LOG2E = 1.4426950408889634

F32 = jnp.float32
BF16 = jnp.bfloat16


def _pick(n, cands):
    for c in cands:
        if n % c == 0:
            return c
    raise ValueError(f"no tile for {n} in {cands}")


def _params(sem, vmem_mib=None):
    return pltpu.CompilerParams(
        dimension_semantics=sem,
        vmem_limit_bytes=None if vmem_mib is None else vmem_mib * 1024 * 1024)


def _rms(x, g):
    ms = jnp.mean(x * x, axis=-1, keepdims=True)
    return (x * lax.rsqrt(ms + RMS_EPS)) * g


def _pool_kernel(*refs, ts, ns, n_real, dg, pos0, aliased):
    if aliased:
        x_ref, halo0_ref, g_ref, w_ref, s_ref, _, out_ref, hist_ref, halo_sc = refs
    else:
        x_ref, halo0_ref, g_ref, w_ref, s_ref, out_ref, hist_ref, halo_sc = refs
    step = pl.program_id(0)
    s = lax.rem(step, ns)

    @pl.when(step >= n_real)
    def _():
        out_ref[...] = jnp.zeros_like(out_ref)

    @pl.when(step < n_real)
    def _():
        @pl.when(s == 0)
        def _():
            halo_sc[...] = halo0_ref[0]

        x = x_ref[...]
        h = _rms(x, g_ref[...])
        halo = halo_sc[...]
        pos = lax.broadcasted_iota(jnp.int32, (ts, 1), 0) + (s * ts + pos0)
        for gi, w in enumerate(POOL_WINDOWS):
            lo, hi = gi * dg, (gi + 1) * dg
            hg = h[:, lo:hi]
            acc = jnp.concatenate([halo[:, lo:hi], hg], axis=0)
            k = 1
            while k < w:
                acc = acc + pltpu.roll(acc, k, 0)
                k *= 2
            cnt = jnp.minimum(pos + 1, w).astype(F32)
            pooled = acc[HALO:, :] * (1.0 / cnt) - hg
            mix = jnp.dot(pooled.astype(BF16), w_ref[gi], preferred_element_type=F32) * s_ref[:, lo:hi]
            out_ref[:, lo:hi] = x[:, lo:hi] + mix
        halo_sc[...] = h[ts - HALO:, :]

        @pl.when(s == ns - 1)
        def _():
            hist_ref[0] = halo_sc[1:HALO, :]


def _pool_layer(x_in, in_off, out_buf, out_off, n_tok_total, batch, seq, halo0, g, w_bf, s, pos0, n_fill_rows=0):
    d = x_in.shape[1]
    dg = d // len(POOL_WINDOWS)
    ts = _pick(seq, (256, 128, 64))
    ns = seq // ts
    assert in_off % ts == 0 and out_off % ts == 0 and n_fill_rows % ts == 0
    ib, ob = in_off // ts, out_off // ts
    n_real = batch * ns
    aliased = out_buf is not None
    halo_b = halo0.shape[0]

    def row(i):
        return jnp.minimum(i, n_real - 1) // ns

    in_specs = [
        pl.BlockSpec((ts, d), lambda i: (ib + jnp.minimum(i, n_real - 1), 0)),
        pl.BlockSpec((1, HALO, d), (lambda i: (row(i), 0, 0)) if halo_b > 1 else (lambda i: (0, 0, 0))),
        pl.BlockSpec((1, d), lambda i: (0, 0)),
        pl.BlockSpec((len(POOL_WINDOWS), dg, dg), lambda i: (0, 0, 0)),
        pl.BlockSpec((1, d), lambda i: (0, 0)),
    ]
    args = [x_in, halo0, g.reshape(1, d), w_bf, s.reshape(1, d)]
    if aliased:
        in_specs.append(pl.BlockSpec(memory_space=pl.ANY))
        args.append(out_buf)
    return pl.pallas_call(
        functools.partial(_pool_kernel, ts=ts, ns=ns, n_real=n_real, dg=dg, pos0=pos0, aliased=aliased),
        grid=(n_real + n_fill_rows // ts,),
        in_specs=in_specs,
        out_specs=[
            pl.BlockSpec((ts, d), lambda i: (ob + i, 0)),
            pl.BlockSpec((1, POOL_HIST, d), lambda i: (row(i), 0, 0)),
        ],
        out_shape=[
            jax.ShapeDtypeStruct((n_tok_total, d), F32),
            jax.ShapeDtypeStruct((batch, POOL_HIST, d), F32),
        ],
        scratch_shapes=[pltpu.VMEM((HALO, d), F32)],
        input_output_aliases={len(args) - 1: 0} if aliased else {},
        compiler_params=_params(("arbitrary",), 48),
        name="pool_layer",
    )(*args)


def _router_kernel(x_ref, g_ref, w_ref, b_ref, route_ref, cnt_ref, carry_sc, *, tm, n_groups, per_group):
    i = pl.program_id(0)

    @pl.when(i == 0)
    def _():
        carry_sc[...] = jnp.zeros_like(carry_sc)

    h = _rms(x_ref[...], g_ref[...])
    lg = jnp.dot(h.astype(BF16), w_ref[...], preferred_element_type=F32) + b_ref[...]
    lane = lax.broadcasted_iota(jnp.int32, (tm, LANES), 1).astype(F32)
    big = float(LANES)

    def first_max(vals, mask):
        v = jnp.max(jnp.where(mask, vals, -jnp.inf), axis=-1, keepdims=True)
        idx = jnp.min(jnp.where(mask & (vals == v), lane, big), axis=-1, keepdims=True)
        return v, idx

    gmask = lane < n_groups
    _, gsel = first_max(lg, gmask)
    lo = n_groups + gsel * per_group
    emask = (lane >= lo) & (lane < lo + per_group)
    _, i1 = first_max(lg, emask)
    _, i2 = first_max(lg, emask & (lane != i1))

    a = jnp.minimum(i1, i2) - lo
    b = jnp.maximum(i1, i2) - lo
    n_pairs = per_group * (per_group - 1) // 2
    key = gsel * n_pairs + a * (2 * per_group - 1 - a) * 0.5 + (b - a - 1.0)
    sel = lane == key
    onehot = jnp.where(sel, 1.0, 0.0)
    r_i = lax.broadcasted_iota(jnp.int32, (tm, tm), 0)
    c_i = lax.broadcasted_iota(jnp.int32, (tm, tm), 1)
    lower = jnp.where(c_i < r_i, 1.0, 0.0).astype(BF16)
    before = jnp.dot(lower, onehot.astype(BF16), preferred_element_type=F32) + carry_sc[...]
    rank = jnp.sum(jnp.where(sel, before, 0.0), axis=-1, keepdims=True)
    carry = carry_sc[...] + jnp.sum(onehot, axis=0, keepdims=True)
    carry_sc[...] = carry
    cnt_ref[...] = carry

    route_ref[...] = jnp.where(lane == 0, key, jnp.where(lane == 1, rank, 0.0))


def _route(x, t, g, w_rt, b_rt, n_groups, per_group):
    d = x.shape[1]
    tm = _pick(t, (512, 256, 128))
    return pl.pallas_call(
        functools.partial(_router_kernel, tm=tm, n_groups=n_groups, per_group=per_group),
        grid=(t // tm,),
        in_specs=[
            pl.BlockSpec((tm, d), lambda i: (i, 0)),
            pl.BlockSpec((1, d), lambda i: (0, 0)),
            pl.BlockSpec((d, LANES), lambda i: (0, 0)),
            pl.BlockSpec((1, LANES), lambda i: (0, 0)),
        ],
        out_specs=[
            pl.BlockSpec((tm, LANES), lambda i: (i, 0)),
            pl.BlockSpec((1, LANES), lambda i: (0, 0)),
        ],
        out_shape=[
            jax.ShapeDtypeStruct((t, LANES), F32),
            jax.ShapeDtypeStruct((1, LANES), F32),
        ],
        scratch_shapes=[pltpu.VMEM((1, LANES), F32)],
        compiler_params=_params(("arbitrary",), 48),
        name="moe_route",
    )(x, g.reshape(1, d), w_rt, b_rt)


def _dispatch_kernel(fill_ref, dest_ref, x_ref, xs_hbm, zbuf, stage, zsem, ssem, *, tm, tb):
    i = pl.program_id(0)
    n = pl.num_programs(0)
    nb = fill_ref.shape[0]
    slot = lax.rem(i, 2)

    def fill_copy(blk):
        return pltpu.make_async_copy(zbuf, xs_hbm.at[pl.ds(pl.multiple_of(blk * tb, tb), tb), :], zsem.at[0])

    def rows_wait(s):
        pltpu.make_async_copy(stage.at[s], xs_hbm.at[pl.ds(0, tm), :], ssem.at[s]).wait()

    @pl.when(i == 0)
    def _():
        zbuf[...] = jnp.zeros_like(zbuf)

        def start(blk, c):
            @pl.when(fill_ref[blk] > 0)
            def _():
                fill_copy(blk).start()
            return c

        def wait(blk, c):
            @pl.when(fill_ref[blk] > 0)
            def _():
                fill_copy(blk).wait()
            return c

        lax.fori_loop(0, nb, start, 0)
        lax.fori_loop(0, nb, wait, 0)

    @pl.when(i >= 2)
    def _():
        rows_wait(slot)

    stage[slot] = x_ref[...]
    for r in range(tm):
        pltpu.make_async_copy(stage.at[slot, pl.ds(r, 1), :], xs_hbm.at[pl.ds(dest_ref[0, 0, r], 1), :],
                              ssem.at[slot]).start()

    @pl.when(i == n - 1)
    def _():
        @pl.when(i >= 1)
        def _():
            rows_wait(1 - slot)
        rows_wait(slot)


def _moe_dispatch(x, t, dest, blk_fill, tb):
    d = x.shape[1]
    tm = _pick(t, (256, 128))
    nt = t // tm
    nb = blk_fill.shape[0]
    grid_spec = pltpu.PrefetchScalarGridSpec(
        num_scalar_prefetch=1,
        grid=(nt,),
        in_specs=[
            pl.BlockSpec((1, 1, tm), lambda i, fill: (i, 0, 0), memory_space=pltpu.SMEM),
            pl.BlockSpec((tm, d), lambda i, fill: (i, 0)),
        ],
        out_specs=pl.BlockSpec(memory_space=pl.ANY),
        scratch_shapes=[pltpu.VMEM((tb, d), F32), pltpu.VMEM((2, tm, d), F32),
                        pltpu.SemaphoreType.DMA((1,)), pltpu.SemaphoreType.DMA((2,))],
    )
    return pl.pallas_call(
        functools.partial(_dispatch_kernel, tm=tm, tb=tb),
        grid_spec=grid_spec,
        out_shape=jax.ShapeDtypeStruct((nb * tb, d), F32),
        compiler_params=_params(("arbitrary",), 48),
        name="moe_dispatch",
    )(blk_fill, dest.reshape(nt, 1, tm), x)


def _moe_kernel(ea_ref, eb_ref, nv_ref, src_ref, dst_ref, xs_ref, g_ref, wrt_ref, brt_ref,
                wga_ref, wua_ref, wda_ref, wgb_ref, wub_ref, wdb_ref, *rest, tb, n_tok, n_groups, final):
    if final:
        gf_ref, out_hbm, obuf, ssem = rest
    else:
        out_hbm, obuf, ssem = rest
    b = pl.program_id(0)
    nb = pl.num_programs(0)
    slot = lax.rem(b, 2)
    nv = nv_ref[b]
    nv_prev2 = nv_ref[jnp.maximum(b - 2, 0)]
    nv_prev1 = nv_ref[jnp.maximum(b - 1, 0)]

    def scatter_wait(s):
        pltpu.make_async_copy(obuf.at[s], out_hbm.at[pl.ds(0, tb), :], ssem.at[s]).wait()

    @pl.when(b == 0)
    def _():
        obuf[...] = jnp.zeros_like(obuf)
        for s in range(2):
            spare = pltpu.make_async_copy(obuf.at[s], out_hbm.at[pl.ds(n_tok + s * tb, tb), :], ssem.at[s])
            spare.start()
            spare.wait()

    def scatter(idx_ref, s):
        for r in range(tb):
            pltpu.make_async_copy(obuf.at[s, pl.ds(r, 1), :], out_hbm.at[pl.ds(idx_ref[0, 0, r], 1), :],
                                  ssem.at[s]).start()

    @pl.when((b >= 2) & (nv_prev2 > 0))
    def _():
        scatter_wait(slot)

    @pl.when(nv > 0)
    def _():
        x = xs_ref[...]
        hn = _rms(x, g_ref[...]).astype(BF16)

        def expert(wg_ref, wu_ref, wd_ref):
            a = jnp.dot(hn, wg_ref[...], preferred_element_type=F32)
            u = jnp.dot(hn, wu_ref[...], preferred_element_type=F32)
            act = (a * (1.0 / (1.0 + jnp.exp(-a)))) * u
            return jnp.dot(act.astype(BF16), wd_ref[...], preferred_element_type=F32)

        lg = jnp.dot(hn, wrt_ref[...], preferred_element_type=F32) + brt_ref[...]
        lane = lax.broadcasted_iota(jnp.int32, (tb, LANES), 1)
        gmask = lane < n_groups
        gmax = jnp.max(jnp.where(gmask, lg, -jnp.inf), axis=-1, keepdims=True)
        g_prob = 1.0 / jnp.sum(jnp.where(gmask, jnp.exp(lg - gmax), 0.0), axis=-1, keepdims=True)
        la = jnp.sum(jnp.where(lane == n_groups + ea_ref[b], lg, 0.0), axis=-1, keepdims=True)
        lb = jnp.sum(jnp.where(lane == n_groups + eb_ref[b], lg, 0.0), axis=-1, keepdims=True)
        top = jnp.maximum(la, lb)
        pa = jnp.exp(la - top)
        pb = jnp.exp(lb - top)
        den = pa + pb
        gate_a = (pa / den) * g_prob
        gate_b = (pb / den) * g_prob

        ya = expert(wga_ref, wua_ref, wda_ref)
        yb = expert(wgb_ref, wub_ref, wdb_ref)
        out = x + (gate_a * ya + gate_b * yb)
        if final:
            out = _rms(out, gf_ref[...])
        obuf[slot] = out
        scatter(dst_ref, slot)

    @pl.when(b == nb - 1)
    def _():
        @pl.when((b >= 1) & (nv_prev1 > 0))
        def _():
            scatter_wait(1 - slot)

        @pl.when(nv > 0)
        def _():
            scatter_wait(slot)


def _moe_mlp(xs, t, g, w_rt, b_rt, n_groups, wg, wu, wd, layer, blk_ea, blk_eb, blk_nv, blk_src, slot_dst, tb,
             g_final):
    d = wg.shape[-2]
    de = wg.shape[-1]
    nb = blk_nv.shape[0]
    dst3 = slot_dst.reshape(nb, 1, tb)
    final = g_final is not None

    def wspec(which, shape):
        return pl.BlockSpec((None, None) + shape, lambda b, ea, eb, nv, src: (layer, (ea, eb)[which][b], 0, 0))

    in_specs = [
        pl.BlockSpec((1, 1, tb), lambda b, ea, eb, nv, src: (b, 0, 0), memory_space=pltpu.SMEM),
        pl.BlockSpec((tb, d), lambda b, ea, eb, nv, src: (src[b], 0)),
        pl.BlockSpec((1, d), lambda b, ea, eb, nv, src: (0, 0)),
        pl.BlockSpec((d, LANES), lambda b, ea, eb, nv, src: (0, 0)),
        pl.BlockSpec((1, LANES), lambda b, ea, eb, nv, src: (0, 0)),
        wspec(0, (d, de)), wspec(0, (d, de)), wspec(0, (de, d)),
        wspec(1, (d, de)), wspec(1, (d, de)), wspec(1, (de, d)),
    ]
    args = [blk_ea, blk_eb, blk_nv, blk_src, dst3, xs, g.reshape(1, d), w_rt, b_rt, wg, wu, wd, wg, wu, wd]
    if final:
        in_specs.append(pl.BlockSpec((1, d), lambda b, ea, eb, nv, src: (0, 0)))
        args.append(g_final.reshape(1, d))
    grid_spec = pltpu.PrefetchScalarGridSpec(
        num_scalar_prefetch=4,
        grid=(nb,),
        in_specs=in_specs,
        out_specs=pl.BlockSpec(memory_space=pl.ANY),
        scratch_shapes=[pltpu.VMEM((2, tb, d), F32), pltpu.SemaphoreType.DMA((2,))],
    )
    return pl.pallas_call(
        functools.partial(_moe_kernel, tb=tb, n_tok=t, n_groups=n_groups, final=final),
        grid_spec=grid_spec,
        out_shape=jax.ShapeDtypeStruct((t + 2 * tb, d), F32),
        compiler_params=_params(("arbitrary",), 56),
        name="moe_mlp",
    )(*args)


def _pair_tables(n_groups, per_group):
    ea, eb = [], []
    for grp in range(n_groups):
        for a in range(per_group):
            for b in range(a + 1, per_group):
                ea.append(grp * per_group + a)
                eb.append(grp * per_group + b)
    return jnp.asarray(ea, jnp.int32), jnp.asarray(eb, jnp.int32)


def _hier_moe(x, t, layer, p, tb, g_final=None):
    n_groups, n_experts = p["n_groups"], p["n_experts"]
    per_group = n_experts // n_groups
    n_buckets = n_groups * (per_group * (per_group - 1) // 2)
    assert n_buckets <= LANES
    route, cnt = _route(x, t, p["g_ffn"][layer], p["w_rt"][layer], p["b_rt"][layer], n_groups, per_group)

    key = route[:, 0].astype(jnp.int32)
    rank = route[:, 1].astype(jnp.int32)
    counts = cnt[0, :n_buckets].astype(jnp.int32)
    padded = (counts + tb - 1) // tb * tb
    pend = jnp.cumsum(padded)
    pstart = pend - padded
    bucket_ids = jnp.arange(n_buckets, dtype=jnp.int32)
    dest = jnp.sum(jnp.where(key[:, None] == bucket_ids[None, :], pstart[None, :], 0), axis=1) + rank
    n_blocks = -(-(t + n_buckets * (tb - 1)) // tb)
    n_slots = n_blocks * tb
    blk_start = jnp.arange(n_blocks, dtype=jnp.int32) * tb
    blk_bucket = jnp.minimum(jnp.sum((pend[None, :] <= blk_start[:, None]).astype(jnp.int32), axis=1), n_buckets - 1)
    blk_nv = jnp.clip(counts[blk_bucket] - (blk_start - pstart[blk_bucket]), 0, tb).astype(jnp.int32)
    tab_a, tab_b = _pair_tables(n_groups, per_group)
    blk_ea, blk_eb = tab_a[blk_bucket], tab_b[blk_bucket]

    slot_tok = jnp.zeros((n_slots,), jnp.int32).at[dest].set(jnp.arange(t, dtype=jnp.int32))
    slot = jnp.arange(n_slots, dtype=jnp.int32)
    in_blk, blk = slot % tb, slot // tb
    valid = in_blk < jnp.repeat(blk_nv, tb)
    slot_dst = jnp.where(valid, slot_tok, t + (blk % 2) * tb + in_blk)
    blk_fill = (blk_nv < tb).astype(jnp.int32)
    n_used = jnp.sum((blk_nv > 0).astype(jnp.int32))
    blk_src = jnp.minimum(jnp.arange(n_blocks, dtype=jnp.int32), jnp.maximum(n_used - 1, 0))

    xs = _moe_dispatch(x, t, dest, blk_fill, tb)
    return _moe_mlp(xs, t, p["g_ffn"][layer], p["w_rt"][layer], p["b_rt"][layer], n_groups, p["w_gate"], p["w_up"],
                    p["w_down"], layer, blk_ea, blk_eb, blk_nv, blk_src, slot_dst, tb, g_final)


def _rms_proj_kernel(x_ref, g_ref, w_ref, *out_refs):
    res = jnp.dot(_rms(x_ref[...], g_ref[...]).astype(BF16), w_ref[...], preferred_element_type=F32)
    for o in out_refs:
        o[...] = res.astype(o.dtype)


def _proj_tiles(t, n):
    return _pick(t, (512, 256, 128)), _pick(n, (2048, 1024, 512, 256, 128))


def _rms_proj(x, t, g, w_bf, out_dtypes):
    d = x.shape[1]
    n = w_bf.shape[1]
    tm, tn = _proj_tiles(t, n)
    return pl.pallas_call(
        _rms_proj_kernel,
        grid=(n // tn, t // tm),
        in_specs=[
            pl.BlockSpec((tm, d), lambda j, i: (i, 0)),
            pl.BlockSpec((1, d), lambda j, i: (0, 0)),
            pl.BlockSpec((d, tn), lambda j, i: (0, j)),
        ],
        out_specs=[pl.BlockSpec((tm, tn), lambda j, i: (i, j)) for _ in out_dtypes],
        out_shape=[jax.ShapeDtypeStruct((t, n), dt) for dt in out_dtypes],
        compiler_params=_params(("parallel", "parallel"), 56),
        name="rms_proj",
    )(x, g.reshape(1, d), w_bf)


def _proj_res_kernel(o_ref, w_ref, x_ref, out_ref):
    out_ref[...] = x_ref[...] + jnp.dot(o_ref[...], w_ref[...], preferred_element_type=F32)


def _proj_res(o_bf, w_bf, x, t):
    d = x.shape[1]
    k = o_bf.shape[1]
    tm, tn = _proj_tiles(t, d)
    return pl.pallas_call(
        _proj_res_kernel,
        grid=(d // tn, t // tm),
        in_specs=[
            pl.BlockSpec((tm, k), lambda j, i: (i, 0)),
            pl.BlockSpec((k, tn), lambda j, i: (0, j)),
            pl.BlockSpec((tm, tn), lambda j, i: (i, j)),
        ],
        out_specs=pl.BlockSpec((tm, tn), lambda j, i: (i, j)),
        out_shape=jax.ShapeDtypeStruct((t, d), F32),
        compiler_params=_params(("parallel", "parallel"), 56),
        name="proj_res",
    )(o_bf, w_bf, x)


def _attn_heads(q_ref, kv_refs, bias_ref, o_ref, i, *, n_seg, n_heads, hd, scale, clamp_mask):
    def seg(ref, h):
        sl = slice(h * hd, (h + 1) * hd)
        return ref[0, :, sl] if len(ref.shape) == 3 else ref[:, sl]

    for h in range(n_heads):
        sl = slice(h * hd, (h + 1) * hd)
        qh = q_ref[:, sl]
        scores = []
        off = 0
        for si in range(n_seg):
            kh = seg(kv_refs[2 * si], h)
            nk = kh.shape[0]
            s = lax.dot_general(qh, kh, (((1,), (1,)), ((), ())), preferred_element_type=F32) * scale
            s = s + bias_ref[h, :, off:off + nk]
            if clamp_mask and si < n_seg - 1:
                s = jnp.where(i >= n_seg - 1 - si, s, NEG_INF)
            scores.append(s)
            off += nk
        m = functools.reduce(jnp.maximum, [jnp.max(s, axis=-1, keepdims=True) for s in scores])
        ps = [jnp.exp2(s - m) for s in scores]
        l = functools.reduce(lambda a, b: a + b, [jnp.sum(p, axis=-1, keepdims=True) for p in ps])
        acc = None
        for si in range(n_seg):
            pv = jnp.dot(ps[si].astype(BF16), seg(kv_refs[2 * si + 1], h), preferred_element_type=F32)
            acc = pv if acc is None else acc + pv
        o_ref[:, sl] = (acc / l).astype(o_ref.dtype)


def _attn_prompt_kernel(*refs, nq, n_real, n_seg, **kw):
    q_ref, kv_refs, bias_ref, o_ref = refs[0], refs[1:1 + 2 * n_seg], refs[1 + 2 * n_seg], refs[-1]
    step = pl.program_id(0)

    @pl.when(step >= n_real)
    def _():
        o_ref[...] = jnp.zeros_like(o_ref)

    @pl.when(step < n_real)
    def _():
        _attn_heads(q_ref, kv_refs, bias_ref, o_ref, lax.rem(step, nq), n_seg=n_seg, **kw)

def _band_bias(table, qb, n_keys, left):
    clip = (table.shape[-1] - 1) // 2
    n_heads = table.shape[0]
    w = qb + n_keys
    m = jnp.arange(w, dtype=jnp.int32)
    delta = jnp.where(m < n_keys, m, m - w)
    u = table[:, jnp.clip(left - delta, -clip, clip) + clip].astype(F32)
    toe = jnp.tile(u, (1, qb))[:, :qb * (w - 1)].reshape(n_heads, qb, w - 1)[:, :, :n_keys]
    i = jnp.arange(qb, dtype=jnp.int32)[:, None]
    j = jnp.arange(n_keys, dtype=jnp.int32)[None, :]
    kpos = j - left
    cq = (i // CHUNK) * CHUNK
    inband = (kpos >= cq - LEFT_LEN) & (kpos < cq + CHUNK)
    return jnp.where(inband[None], toe * LOG2E, NEG_INF)


def _attn_prompt(q, kv, table, batch, seq, n_heads, n_tok_total):
    d = q.shape[1]
    hd = d // n_heads
    qb = _pick(seq, (256, 128, 64))
    nq = seq // qb
    n_seg = LEFT_LEN // qb + 1
    n_real = batch * nq
    n_fill_rows = n_tok_total - batch * seq
    assert n_fill_rows % qb == 0
    bias = _band_bias(table, qb, n_seg * qb, LEFT_LEN)

    def kv_spec(si, col):
        back = n_seg - 1 - si

        def index(step):
            r = jnp.minimum(step, n_real - 1)
            return (r - jnp.minimum(r % nq, back), col)
        return pl.BlockSpec((qb, d), index)

    in_specs = [pl.BlockSpec((qb, d), lambda step: (jnp.minimum(step, n_real - 1), 0))]
    args = [q]
    for si in range(n_seg):
        in_specs += [kv_spec(si, 0), kv_spec(si, 1)]
        args += [kv, kv]
    in_specs.append(pl.BlockSpec((n_heads, qb, n_seg * qb), lambda step: (0, 0, 0)))
    args.append(bias)
    return pl.pallas_call(
        functools.partial(_attn_prompt_kernel, nq=nq, n_real=n_real, n_seg=n_seg, n_heads=n_heads, hd=hd,
                          scale=hd ** -0.5 * LOG2E, clamp_mask=True),
        grid=(n_real + n_fill_rows // qb,),
        in_specs=in_specs,
        out_specs=pl.BlockSpec((qb, d), lambda step: (step, 0)),
        out_shape=jax.ShapeDtypeStruct((n_tok_total, d), BF16),
        compiler_params=_params(("arbitrary",), 56),
        name="attn_prompt",
    )(*args)


def _attn_sample_kernel(*refs, n_seg, **kw):
    _attn_heads(refs[0], refs[1:1 + 2 * n_seg], refs[1 + 2 * n_seg], refs[-1], 0, n_seg=n_seg, **kw)


def _attn_sample(q, kv, cache_k, cache_v, table, row0, batch, seq, n_heads, o_buf):
    d = q.shape[1]
    hd = d // n_heads
    past = cache_k.shape[1]
    assert row0 % seq == 0
    rb = row0 // seq
    assert seq <= CHUNK and past == LEFT_LEN
    bias = _band_bias(table, seq, past + seq, past)
    in_specs = [
        pl.BlockSpec((seq, d), lambda b, i: (rb + b, 0)),
        pl.BlockSpec((1, past, d), lambda b, i: (b, 0, 0)),
        pl.BlockSpec((1, past, d), lambda b, i: (b, 0, 0)),
        pl.BlockSpec((seq, d), lambda b, i: (rb + b, 0)),
        pl.BlockSpec((seq, d), lambda b, i: (rb + b, 1)),
        pl.BlockSpec((n_heads, seq, past + seq), lambda b, i: (0, 0, 0)),
        pl.BlockSpec(memory_space=pl.ANY),
    ]
    return pl.pallas_call(
        functools.partial(_attn_sample_kernel, n_seg=2, n_heads=n_heads, hd=hd, scale=hd ** -0.5 * LOG2E,
                          clamp_mask=False),
        grid=(batch, 1),
        in_specs=in_specs,
        out_specs=pl.BlockSpec((seq, d), lambda b, i: (rb + b, 0)),
        out_shape=jax.ShapeDtypeStruct(o_buf.shape, BF16),
        input_output_aliases={6: 0},
        compiler_params=_params(("parallel", "arbitrary"), 56),
        name="attn_sample",
    )(q, cache_k, cache_v, kv, kv, bias, o_buf)


def kernel(x_prompt, x_sample, state_pool, cache_k, cache_v, g_pool, w_pool, s_pool, g_kv, w_kv, g_attn, w_q, w_o, rel_bias, g_ffn, w_group, b_group, w_router, b_router, w_gate, w_up, w_down, g_final):
    bp, lp, d = x_prompt.shape
    bs, ls, _ = x_sample.shape
    depth = g_ffn.shape[0]
    n_pool = g_pool.shape[0]
    n_heads, hd = cache_k.shape[2], cache_k.shape[3]
    n_groups, n_experts = w_group.shape[-1], w_router.shape[-1]
    tp, tsm = bp * lp, bs * ls
    t = tp + tsm
    tb = 128

    w_rt = jnp.concatenate([w_group, w_router], axis=-1)
    w_rt = jnp.pad(w_rt, ((0, 0), (0, 0), (0, LANES - w_rt.shape[-1]))).astype(BF16)
    b_rt = jnp.concatenate([b_group, b_router], axis=-1)
    b_rt = jnp.pad(b_rt, ((0, 0), (0, LANES - b_rt.shape[-1])))[:, None, :]
    moe_p = dict(n_groups=n_groups, n_experts=n_experts, g_ffn=g_ffn, w_rt=w_rt, b_rt=b_rt,
                 w_gate=w_gate.astype(BF16), w_up=w_up.astype(BF16), w_down=w_down.astype(BF16))
    w_pool_bf = w_pool.astype(BF16)
    w_kv_bf = w_kv.astype(BF16)
    w_q_bf = w_q.astype(BF16)
    w_o_bf = w_o.astype(BF16)

    zero_halo = jnp.zeros((1, HALO, d), F32)
    samp_halo = jnp.pad(state_pool, ((0, 0), (0, 0), (1, 0), (0, 0)))

    ck = cache_k.reshape(bs, cache_k.shape[1], d).astype(BF16)
    cv = cache_v.reshape(bs, cache_v.shape[1], d).astype(BF16)

    x = None
    hist_p, hist_s = [], []
    kv_f32 = kv_bf = None
    for layer in range(depth):
        if layer < n_pool:
            if layer == 0:
                src_p, off_p, src_s, off_s = x_prompt.reshape(tp, d), 0, x_sample.reshape(tsm, d), 0
            else:
                src_p, off_p, src_s, off_s = x, 0, x, tp
            buf, hp = _pool_layer(src_p, off_p, None, 0, t, bp, lp, zero_halo, g_pool[layer], w_pool_bf[layer],
                                  s_pool[layer], 0, n_fill_rows=tsm)
            x, hs = _pool_layer(src_s, off_s, buf, tp, t, bs, ls, samp_halo[layer], g_pool[layer], w_pool_bf[layer],
                                s_pool[layer], PAST_LEN)
            hist_p.append(hp)
            hist_s.append(hs)
        else:
            j = layer - n_pool
            if j == 0:
                kv_f32, kv_bf = _rms_proj(x, t, g_kv, w_kv_bf, (F32, BF16))
            (q,) = _rms_proj(x, t, g_attn[j], w_q_bf[j], (BF16,))
            o = _attn_prompt(q, kv_bf, rel_bias[j], bp, lp, n_heads, t)
            o = _attn_sample(q, kv_bf, ck, cv, rel_bias[j], tp, bs, ls, n_heads, o)
            x = _proj_res(o, w_o_bf[j], x, t)
        x = _hier_moe(x, t, layer, moe_p, tb, g_final if layer == depth - 1 else None)

    y_prompt = x[:tp].reshape(bp, lp, d)
    y_sample = x[tp:t].reshape(bs, ls, d)
    keep = min(LEFT_LEN, lp)
    if tsm % lp == 0:
        kvp = kv_f32.reshape(t // lp, lp, 2 * d)[:bp, lp - keep:]
    else:
        kvp = kv_f32[:tp].reshape(bp, lp, 2 * d)[:, lp - keep:]
    kvs = kv_f32[tp:].reshape(bs, ls, 2 * d)
    return (y_prompt, y_sample, jnp.stack(hist_p, axis=0), jnp.stack(hist_s, axis=0),
            kvp[..., :d].reshape(bp, keep, n_heads, hd), kvp[..., d:].reshape(bp, keep, n_heads, hd),
            kvs[..., :d].reshape(bs, ls, n_heads, hd), kvs[..., d:].reshape(bs, ls, n_heads, hd))
```

```python
import functools

import jax
import jax.numpy as jnp
from jax import lax
from jax.experimental import pallas as pl
from jax.experimental.pallas import tpu as pltpu

RMS_EPS = 1e-6
NEG_INF = -1e30
CHUNK = 64
LEFT_CHUNKS = 8
LEFT_LEN = LEFT_CHUNKS * CHUNK
POOL_WINDOWS = (2, 4, 8, 16)
POOL_HIST = max(POOL_WINDOWS) - 1
HALO = POOL_HIST + 1
PAST_LEN = 1024
LANES = 128
LOG2E = 1.4426950408889634

F32 = jnp.float32
BF16 = jnp.bfloat16


def _pick(n, cands):
    for c in cands:
        if n % c == 0:
            return c
    raise ValueError(f"no tile for {n} in {cands}")


def _params(sem, vmem_mib=None):
    return pltpu.CompilerParams(
        dimension_semantics=sem,
        vmem_limit_bytes=None if vmem_mib is None else vmem_mib * 1024 * 1024)


def _rms(x, g):
    ms = jnp.mean(x * x, axis=-1, keepdims=True)
    return (x * lax.rsqrt(ms + RMS_EPS)) * g


def _pool_kernel(*refs, ts, ns, n_real, dg, pos0, aliased):
    if aliased:
        x_ref, halo0_ref, g_ref, w_ref, s_ref, _, out_ref, hist_ref, halo_sc = refs
    else:
        x_ref, halo0_ref, g_ref, w_ref, s_ref, out_ref, hist_ref, halo_sc = refs
    step = pl.program_id(0)
    s = lax.rem(step, ns)

    @pl.when(step >= n_real)
    def _():
        out_ref[...] = jnp.zeros_like(out_ref)

    @pl.when(step < n_real)
    def _():
        @pl.when(s == 0)
        def _():
            halo_sc[...] = halo0_ref[0]

        x = x_ref[...]
        h = _rms(x, g_ref[...])
        halo = halo_sc[...]
        pos = lax.broadcasted_iota(jnp.int32, (ts, 1), 0) + (s * ts + pos0)
        for gi, w in enumerate(POOL_WINDOWS):
            lo, hi = gi * dg, (gi + 1) * dg
            hg = h[:, lo:hi]
            acc = jnp.concatenate([halo[:, lo:hi], hg], axis=0)
            k = 1
            while k < w:
                acc = acc + pltpu.roll(acc, k, 0)
                k *= 2
            cnt = jnp.minimum(pos + 1, w).astype(F32)
            pooled = acc[HALO:, :] * (1.0 / cnt) - hg
            mix = jnp.dot(pooled.astype(BF16), w_ref[gi], preferred_element_type=F32) * s_ref[:, lo:hi]
            out_ref[:, lo:hi] = x[:, lo:hi] + mix
        halo_sc[...] = h[ts - HALO:, :]

        @pl.when(s == ns - 1)
        def _():
            hist_ref[0] = halo_sc[1:HALO, :]


def _pool_layer(x_in, in_off, out_buf, out_off, n_tok_total, batch, seq, halo0, g, w_bf, s, pos0, n_fill_rows=0):
    d = x_in.shape[1]
    dg = d // len(POOL_WINDOWS)
    ts = _pick(seq, (256, 128, 64))
    ns = seq // ts
    assert in_off % ts == 0 and out_off % ts == 0 and n_fill_rows % ts == 0
    ib, ob = in_off // ts, out_off // ts
    n_real = batch * ns
    aliased = out_buf is not None
    halo_b = halo0.shape[0]

    def row(i):
        return jnp.minimum(i, n_real - 1) // ns

    in_specs = [
        pl.BlockSpec((ts, d), lambda i: (ib + jnp.minimum(i, n_real - 1), 0)),
        pl.BlockSpec((1, HALO, d), (lambda i: (row(i), 0, 0)) if halo_b > 1 else (lambda i: (0, 0, 0))),
        pl.BlockSpec((1, d), lambda i: (0, 0)),
        pl.BlockSpec((len(POOL_WINDOWS), dg, dg), lambda i: (0, 0, 0)),
        pl.BlockSpec((1, d), lambda i: (0, 0)),
    ]
    args = [x_in, halo0, g.reshape(1, d), w_bf, s.reshape(1, d)]
    if aliased:
        in_specs.append(pl.BlockSpec(memory_space=pl.ANY))
        args.append(out_buf)
    return pl.pallas_call(
        functools.partial(_pool_kernel, ts=ts, ns=ns, n_real=n_real, dg=dg, pos0=pos0, aliased=aliased),
        grid=(n_real + n_fill_rows // ts,),
        in_specs=in_specs,
        out_specs=[
            pl.BlockSpec((ts, d), lambda i: (ob + i, 0)),
            pl.BlockSpec((1, POOL_HIST, d), lambda i: (row(i), 0, 0)),
        ],
        out_shape=[
            jax.ShapeDtypeStruct((n_tok_total, d), F32),
            jax.ShapeDtypeStruct((batch, POOL_HIST, d), F32),
        ],
        scratch_shapes=[pltpu.VMEM((HALO, d), F32)],
        input_output_aliases={len(args) - 1: 0} if aliased else {},
        compiler_params=_params(("arbitrary",), 48),
        name="pool_layer",
    )(*args)


def _router_kernel(x_ref, g_ref, w_ref, b_ref, route_ref, cnt_ref, carry_sc, *, tm, n_groups, per_group):
    @pl.when(pl.program_id(0) == 0)
    def _():
        carry_sc[...] = jnp.zeros_like(carry_sc)

    h = _rms(x_ref[...], g_ref[...])
    lg = jnp.dot(h.astype(BF16), w_ref[...], preferred_element_type=F32) + b_ref[...]
    lane = lax.broadcasted_iota(jnp.int32, (tm, LANES), 1).astype(F32)
    big = float(LANES)

    def first_max(vals, mask):
        v = jnp.max(jnp.where(mask, vals, -jnp.inf), axis=-1, keepdims=True)
        idx = jnp.min(jnp.where(mask & (vals == v), lane, big), axis=-1, keepdims=True)
        return v, idx

    gmask = lane < n_groups
    _, gsel = first_max(lg, gmask)
    lo = n_groups + gsel * per_group
    emask = (lane >= lo) & (lane < lo + per_group)
    _, i1 = first_max(lg, emask)
    _, i2 = first_max(lg, emask & (lane != i1))

    a = jnp.minimum(i1, i2) - lo
    b = jnp.maximum(i1, i2) - lo
    n_pairs = per_group * (per_group - 1) // 2
    key = gsel * n_pairs + a * (2 * per_group - 1 - a) * 0.5 + (b - a - 1.0)
    sel = lane == key
    onehot = jnp.where(sel, 1.0, 0.0)
    r_i = lax.broadcasted_iota(jnp.int32, (tm, tm), 0)
    c_i = lax.broadcasted_iota(jnp.int32, (tm, tm), 1)
    lower = jnp.where(c_i < r_i, 1.0, 0.0).astype(BF16)
    before = jnp.dot(lower, onehot.astype(BF16), preferred_element_type=F32) + carry_sc[...]
    rank = jnp.sum(jnp.where(sel, before, 0.0), axis=-1, keepdims=True)
    carry = carry_sc[...] + jnp.sum(onehot, axis=0, keepdims=True)
    carry_sc[...] = carry
    cnt_ref[...] = carry

    route_ref[...] = jnp.where(lane == 0, key, jnp.where(lane == 1, rank, 0.0))


def _route(x, t, g, w_rt, b_rt, n_groups, per_group):
    d = x.shape[1]
    tm = _pick(t, (512, 256, 128))
    return pl.pallas_call(
        functools.partial(_router_kernel, tm=tm, n_groups=n_groups, per_group=per_group),
        grid=(t // tm,),
        in_specs=[
            pl.BlockSpec((tm, d), lambda i: (i, 0)),
            pl.BlockSpec((1, d), lambda i: (0, 0)),
            pl.BlockSpec((d, LANES), lambda i: (0, 0)),
            pl.BlockSpec((1, LANES), lambda i: (0, 0)),
        ],
        out_specs=[
            pl.BlockSpec((tm, LANES), lambda i: (i, 0)),
            pl.BlockSpec((1, LANES), lambda i: (0, 0)),
        ],
        out_shape=[
            jax.ShapeDtypeStruct((t, LANES), F32),
            jax.ShapeDtypeStruct((1, LANES), F32),
        ],
        scratch_shapes=[pltpu.VMEM((1, LANES), F32)],
        compiler_params=_params(("arbitrary",), 48),
        name="moe_route",
    )(x, g.reshape(1, d), w_rt, b_rt)


def _dispatch_kernel(fill_ref, dest_ref, x_ref, xs_hbm, zbuf, stage, zsem, ssem, *, tm, tb):
    i = pl.program_id(0)
    n = pl.num_programs(0)
    nb = fill_ref.shape[0]
    slot = lax.rem(i, 2)

    def fill_copy(blk):
        return pltpu.make_async_copy(zbuf, xs_hbm.at[pl.ds(pl.multiple_of(blk * tb, tb), tb), :], zsem.at[0])

    def rows_wait(s):
        pltpu.make_async_copy(stage.at[s], xs_hbm.at[pl.ds(0, tm), :], ssem.at[s]).wait()

    @pl.when(i == 0)
    def _():
        zbuf[...] = jnp.zeros_like(zbuf)

        def start(blk, c):
            @pl.when(fill_ref[blk] > 0)
            def _():
                fill_copy(blk).start()
            return c

        def wait(blk, c):
            @pl.when(fill_ref[blk] > 0)
            def _():
                fill_copy(blk).wait()
            return c

        lax.fori_loop(0, nb, start, 0)
        lax.fori_loop(0, nb, wait, 0)

    @pl.when(i >= 2)
    def _():
        rows_wait(slot)

    stage[slot] = x_ref[...]
    for r in range(tm):
        pltpu.make_async_copy(stage.at[slot, pl.ds(r, 1), :], xs_hbm.at[pl.ds(dest_ref[0, 0, r], 1), :],
                              ssem.at[slot]).start(priority=r % 2)

    @pl.when(i == n - 1)
    def _():
        @pl.when(i >= 1)
        def _():
            rows_wait(1 - slot)
        rows_wait(slot)


def _moe_dispatch(x, t, dest, blk_fill, tb):
    d = x.shape[1]
    tm = _pick(t, (256, 128))
    nt = t // tm
    nb = blk_fill.shape[0]
    grid_spec = pltpu.PrefetchScalarGridSpec(
        num_scalar_prefetch=1,
        grid=(nt,),
        in_specs=[
            pl.BlockSpec((1, 1, tm), lambda i, fill: (i, 0, 0), memory_space=pltpu.SMEM),
            pl.BlockSpec((tm, d), lambda i, fill: (i, 0)),
        ],
        out_specs=pl.BlockSpec(memory_space=pl.ANY),
        scratch_shapes=[pltpu.VMEM((tb, d), F32), pltpu.VMEM((2, tm, d), F32),
                        pltpu.SemaphoreType.DMA((1,)), pltpu.SemaphoreType.DMA((2,))],
    )
    return pl.pallas_call(
        functools.partial(_dispatch_kernel, tm=tm, tb=tb),
        grid_spec=grid_spec,
        out_shape=jax.ShapeDtypeStruct((nb * tb, d), F32),
        compiler_params=_params(("arbitrary",), 48),
        name="moe_dispatch",
    )(blk_fill, dest.reshape(nt, 1, tm), x)


def _moe_kernel(ea_ref, eb_ref, nv_ref, src_ref, dst_ref, xs_ref, g_ref, wrt_ref, brt_ref,
                wga_ref, wua_ref, wda_ref, wgb_ref, wub_ref, wdb_ref, *rest, tb, n_tok, n_groups, final):
    if final:
        gf_ref, out_hbm, obuf, ssem = rest
    else:
        out_hbm, obuf, ssem = rest
    b = pl.program_id(0)
    nb = pl.num_programs(0)
    slot = lax.rem(b, 2)
    nv = nv_ref[b]
    nv_prev2 = nv_ref[jnp.maximum(b - 2, 0)]
    nv_prev1 = nv_ref[jnp.maximum(b - 1, 0)]

    def scatter_wait(s):
        pltpu.make_async_copy(obuf.at[s], out_hbm.at[pl.ds(0, tb), :], ssem.at[s]).wait()

    @pl.when(b == 0)
    def _():
        obuf[...] = jnp.zeros_like(obuf)
        for s in range(2):
            spare = pltpu.make_async_copy(obuf.at[s], out_hbm.at[pl.ds(n_tok + s * tb, tb), :], ssem.at[s])
            spare.start()
            spare.wait()

    def scatter(idx_ref, s):
        for r in range(tb):
            pltpu.make_async_copy(obuf.at[s, pl.ds(r, 1), :], out_hbm.at[pl.ds(idx_ref[0, 0, r], 1), :],
                                  ssem.at[s]).start()

    @pl.when((b >= 2) & (nv_prev2 > 0))
    def _():
        scatter_wait(slot)

    @pl.when(nv > 0)
    def _():
        x = xs_ref[...]
        hn = _rms(x, g_ref[...]).astype(BF16)

        def expert(wg_ref, wu_ref, wd_ref):
            a = jnp.dot(hn, wg_ref[...], preferred_element_type=F32)
            u = jnp.dot(hn, wu_ref[...], preferred_element_type=F32)
            act = (a * (1.0 / (1.0 + jnp.exp(-a)))) * u
            return jnp.dot(act.astype(BF16), wd_ref[...], preferred_element_type=F32)

        lg = jnp.dot(hn, wrt_ref[...], preferred_element_type=F32) + brt_ref[...]
        lane = lax.broadcasted_iota(jnp.int32, (tb, LANES), 1)
        gmask = lane < n_groups
        gmax = jnp.max(jnp.where(gmask, lg, -jnp.inf), axis=-1, keepdims=True)
        g_prob = 1.0 / jnp.sum(jnp.where(gmask, jnp.exp(lg - gmax), 0.0), axis=-1, keepdims=True)
        la = jnp.sum(jnp.where(lane == n_groups + ea_ref[b], lg, 0.0), axis=-1, keepdims=True)
        lb = jnp.sum(jnp.where(lane == n_groups + eb_ref[b], lg, 0.0), axis=-1, keepdims=True)
        top = jnp.maximum(la, lb)
        pa = jnp.exp(la - top)
        pb = jnp.exp(lb - top)
        den = pa + pb
        gate_a = (pa / den) * g_prob
        gate_b = (pb / den) * g_prob

        ya = expert(wga_ref, wua_ref, wda_ref)
        yb = expert(wgb_ref, wub_ref, wdb_ref)
        out = x + (gate_a * ya + gate_b * yb)
        if final:
            out = _rms(out, gf_ref[...])
        obuf[slot] = out
        scatter(dst_ref, slot)

    @pl.when(b == nb - 1)
    def _():
        @pl.when((b >= 1) & (nv_prev1 > 0))
        def _():
            scatter_wait(1 - slot)

        @pl.when(nv > 0)
        def _():
            scatter_wait(slot)


def _moe_mlp(xs, t, g, w_rt, b_rt, n_groups, wg, wu, wd, layer, blk_ea, blk_eb, blk_nv, blk_src, slot_dst, tb,
             g_final):
    d = wg.shape[-2]
    de = wg.shape[-1]
    nb = blk_nv.shape[0]
    dst3 = slot_dst.reshape(nb, 1, tb)
    final = g_final is not None

    def wspec(which, shape):
        return pl.BlockSpec((None, None) + shape, lambda b, ea, eb, nv, src: (layer, (ea, eb)[which][b], 0, 0))

    in_specs = [
        pl.BlockSpec((1, 1, tb), lambda b, ea, eb, nv, src: (b, 0, 0), memory_space=pltpu.SMEM),
        pl.BlockSpec((tb, d), lambda b, ea, eb, nv, src: (src[b], 0)),
        pl.BlockSpec((1, d), lambda b, ea, eb, nv, src: (0, 0)),
        pl.BlockSpec((d, LANES), lambda b, ea, eb, nv, src: (0, 0)),
        pl.BlockSpec((1, LANES), lambda b, ea, eb, nv, src: (0, 0)),
        wspec(0, (d, de)), wspec(0, (d, de)), wspec(0, (de, d)),
        wspec(1, (d, de)), wspec(1, (d, de)), wspec(1, (de, d)),
    ]
    args = [blk_ea, blk_eb, blk_nv, blk_src, dst3, xs, g.reshape(1, d), w_rt, b_rt, wg, wu, wd, wg, wu, wd]
    if final:
        in_specs.append(pl.BlockSpec((1, d), lambda b, ea, eb, nv, src: (0, 0)))
        args.append(g_final.reshape(1, d))
    grid_spec = pltpu.PrefetchScalarGridSpec(
        num_scalar_prefetch=4,
        grid=(nb,),
        in_specs=in_specs,
        out_specs=pl.BlockSpec(memory_space=pl.ANY),
        scratch_shapes=[pltpu.VMEM((2, tb, d), F32), pltpu.SemaphoreType.DMA((2,))],
    )
    return pl.pallas_call(
        functools.partial(_moe_kernel, tb=tb, n_tok=t, n_groups=n_groups, final=final),
        grid_spec=grid_spec,
        out_shape=jax.ShapeDtypeStruct((t + 2 * tb, d), F32),
        compiler_params=_params(("arbitrary",), 56),
        name="moe_mlp",
    )(*args)


def _pair_tables(n_groups, per_group):
    ea, eb = [], []
    for grp in range(n_groups):
        for a in range(per_group):
            for b in range(a + 1, per_group):
                ea.append(grp * per_group + a)
                eb.append(grp * per_group + b)
    return jnp.asarray(ea, jnp.int32), jnp.asarray(eb, jnp.int32)


def _hier_moe(x, t, layer, p, tb, g_final=None):
    n_groups, n_experts = p["n_groups"], p["n_experts"]
    per_group = n_experts // n_groups
    n_buckets = n_groups * (per_group * (per_group - 1) // 2)
    assert n_buckets <= LANES
    route, cnt = _route(x, t, p["g_ffn"][layer], p["w_rt"][layer], p["b_rt"][layer], n_groups, per_group)

    key = route[:, 0].astype(jnp.int32)
    rank = route[:, 1].astype(jnp.int32)
    counts = cnt[0, :n_buckets].astype(jnp.int32)
    padded = (counts + tb - 1) // tb * tb
    pend = jnp.cumsum(padded)
    pstart = pend - padded
    bucket_ids = jnp.arange(n_buckets, dtype=jnp.int32)
    dest = jnp.sum(jnp.where(key[:, None] == bucket_ids[None, :], pstart[None, :], 0), axis=1) + rank
    n_blocks = -(-(t + n_buckets * (tb - 1)) // tb)
    n_slots = n_blocks * tb
    blk_start = jnp.arange(n_blocks, dtype=jnp.int32) * tb
    blk_bucket = jnp.minimum(jnp.sum((pend[None, :] <= blk_start[:, None]).astype(jnp.int32), axis=1), n_buckets - 1)
    blk_nv = jnp.clip(counts[blk_bucket] - (blk_start - pstart[blk_bucket]), 0, tb).astype(jnp.int32)
    tab_a, tab_b = _pair_tables(n_groups, per_group)
    blk_ea, blk_eb = tab_a[blk_bucket], tab_b[blk_bucket]

    slot_tok = jnp.zeros((n_slots,), jnp.int32).at[dest].set(jnp.arange(t, dtype=jnp.int32))
    slot = jnp.arange(n_slots, dtype=jnp.int32)
    in_blk, blk = slot % tb, slot // tb
    valid = in_blk < jnp.repeat(blk_nv, tb)
    slot_dst = jnp.where(valid, slot_tok, t + (blk % 2) * tb + in_blk)
    blk_fill = (blk_nv < tb).astype(jnp.int32)
    n_used = jnp.sum((blk_nv > 0).astype(jnp.int32))
    blk_src = jnp.minimum(jnp.arange(n_blocks, dtype=jnp.int32), jnp.maximum(n_used - 1, 0))

    xs = _moe_dispatch(x, t, dest, blk_fill, tb)
    return _moe_mlp(xs, t, p["g_ffn"][layer], p["w_rt"][layer], p["b_rt"][layer], n_groups, p["w_gate"], p["w_up"],
                    p["w_down"], layer, blk_ea, blk_eb, blk_nv, blk_src, slot_dst, tb, g_final)


def _rms_proj_kernel(x_ref, g_ref, w_ref, *out_refs):
    res = jnp.dot(_rms(x_ref[...], g_ref[...]).astype(BF16), w_ref[...], preferred_element_type=F32)
    for o in out_refs:
        o[...] = res.astype(o.dtype)


def _proj_tiles(t, n):
    return _pick(t, (512, 256, 128)), _pick(n, (2048, 1024, 512, 256, 128))


def _rms_proj(x, t, g, w_bf, out_dtypes, keep=None):
    d = x.shape[1]
    n = w_bf.shape[1]
    tm, tn = _proj_tiles(t, n)
    full_map = lambda j, i: (i, j)
    keep_map, keep_rows = full_map, t
    if keep is not None:
        n_seq, seq_len, keep_len = keep
        if seq_len % tm == 0 and keep_len % tm == 0:
            per_seq, kept = seq_len // tm, keep_len // tm
            n_seq_tiles = n_seq * per_seq
            keep_rows = n_seq * keep_len + (t - n_seq * seq_len)

            def keep_map(j, i):
                in_seq = (i // per_seq) * kept + jnp.maximum(i % per_seq - (per_seq - kept), 0)
                return (jnp.where(i < n_seq_tiles, in_seq, n_seq * kept + (i - n_seq_tiles)), j)
    specs, shapes = [], []
    for idx, dt in enumerate(out_dtypes):
        compact = idx == 0 and keep_rows != t
        specs.append(pl.BlockSpec((tm, tn), keep_map if compact else full_map))
        shapes.append(jax.ShapeDtypeStruct((keep_rows if compact else t, n), dt))
    return pl.pallas_call(
        _rms_proj_kernel,
        grid=(n // tn, t // tm),
        in_specs=[
            pl.BlockSpec((tm, d), lambda j, i: (i, 0)),
            pl.BlockSpec((1, d), lambda j, i: (0, 0)),
            pl.BlockSpec((d, tn), lambda j, i: (0, j)),
        ],
        out_specs=specs,
        out_shape=shapes,
        compiler_params=_params(("arbitrary", "arbitrary"), 56),
        name="rms_proj",
    )(x, g.reshape(1, d), w_bf)


def _proj_res_kernel(o_ref, w_ref, x_ref, out_ref):
    out_ref[...] = x_ref[...] + jnp.dot(o_ref[...], w_ref[...], preferred_element_type=F32)


def _proj_res(o_bf, w_bf, x, t):
    d = x.shape[1]
    k = o_bf.shape[1]
    tm, tn = _proj_tiles(t, d)
    return pl.pallas_call(
        _proj_res_kernel,
        grid=(d // tn, t // tm),
        in_specs=[
            pl.BlockSpec((tm, k), lambda j, i: (i, 0)),
            pl.BlockSpec((k, tn), lambda j, i: (0, j)),
            pl.BlockSpec((tm, tn), lambda j, i: (i, j)),
        ],
        out_specs=pl.BlockSpec((tm, tn), lambda j, i: (i, j)),
        out_shape=jax.ShapeDtypeStruct((t, d), F32),
        compiler_params=_params(("parallel", "parallel"), 56),
        name="proj_res",
    )(o_bf, w_bf, x)


def _attn_heads(q_ref, kv_refs, bias_ref, o_ref, i, *, n_seg, n_heads, hd, scale, clamp_mask):
    def seg(ref, h):
        sl = slice(h * hd, (h + 1) * hd)
        return ref[0, :, sl] if len(ref.shape) == 3 else ref[:, sl]

    for h in range(n_heads):
        sl = slice(h * hd, (h + 1) * hd)
        qh = q_ref[:, sl]
        scores = []
        off = 0
        for si in range(n_seg):
            kh = seg(kv_refs[2 * si], h)
            nk = kh.shape[0]
            s = lax.dot_general(qh, kh, (((1,), (1,)), ((), ())), preferred_element_type=F32) * scale
            s = s + bias_ref[h, :, off:off + nk]
            if clamp_mask and si < n_seg - 1:
                s = jnp.where(i >= n_seg - 1 - si, s, NEG_INF)
            scores.append(s)
            off += nk
        m = functools.reduce(jnp.maximum, [jnp.max(s, axis=-1, keepdims=True) for s in scores])
        ps = [jnp.exp2(s - m) for s in scores]
        l = functools.reduce(lambda a, b: a + b, [jnp.sum(p, axis=-1, keepdims=True) for p in ps])
        acc = None
        for si in range(n_seg):
            pv = jnp.dot(ps[si].astype(BF16), seg(kv_refs[2 * si + 1], h), preferred_element_type=F32)
            acc = pv if acc is None else acc + pv
        o_ref[:, sl] = (acc / l).astype(o_ref.dtype)


def _attn_prompt_kernel(*refs, nq, n_real, n_seg, **kw):
    q_ref, kv_refs, bias_ref, o_ref = refs[0], refs[1:1 + 2 * n_seg], refs[1 + 2 * n_seg], refs[-1]
    step = pl.program_id(0)

    @pl.when(step >= n_real)
    def _():
        o_ref[...] = jnp.zeros_like(o_ref)

    @pl.when(step < n_real)
    def _():
        _attn_heads(q_ref, kv_refs, bias_ref, o_ref, lax.rem(step, nq), n_seg=n_seg, clamp_mask=True, **kw)

def _band_bias(table, qb, n_keys, left):
    clip = (table.shape[-1] - 1) // 2
    n_heads = table.shape[0]
    w = qb + n_keys
    m = jnp.arange(w, dtype=jnp.int32)
    delta = jnp.where(m < n_keys, m, m - w)
    u = table[:, jnp.clip(left - delta, -clip, clip) + clip].astype(F32)
    toe = jnp.tile(u, (1, qb))[:, :qb * (w - 1)].reshape(n_heads, qb, w - 1)[:, :, :n_keys]
    i = jnp.arange(qb, dtype=jnp.int32)[:, None]
    j = jnp.arange(n_keys, dtype=jnp.int32)[None, :]
    kpos = j - left
    cq = (i // CHUNK) * CHUNK
    inband = (kpos >= cq - LEFT_LEN) & (kpos < cq + CHUNK)
    return jnp.where(inband[None], toe * LOG2E, NEG_INF)


def _attn_prompt(q, kv, table, batch, seq, n_heads, n_tok_total):
    d = q.shape[1]
    hd = d // n_heads
    qb = _pick(seq, (256, 128, 64))
    nq = seq // qb
    n_seg = LEFT_LEN // qb + 1
    n_real = batch * nq
    n_fill_rows = n_tok_total - batch * seq
    assert n_fill_rows % qb == 0
    bias = _band_bias(table, qb, n_seg * qb, LEFT_LEN)

    def kv_spec(si, col):
        back = n_seg - 1 - si

        def index(step):
            r = jnp.minimum(step, n_real - 1)
            return (r - jnp.minimum(r % nq, back), col)
        return pl.BlockSpec((qb, d), index)

    in_specs = [pl.BlockSpec((qb, d), lambda step: (jnp.minimum(step, n_real - 1), 0))]
    args = [q]
    for si in range(n_seg):
        in_specs += [kv_spec(si, 0), kv_spec(si, 1)]
        args += [kv, kv]
    in_specs.append(pl.BlockSpec((n_heads, qb, n_seg * qb), lambda step: (0, 0, 0)))
    args.append(bias)
    return pl.pallas_call(
        functools.partial(_attn_prompt_kernel, nq=nq, n_real=n_real, n_seg=n_seg, n_heads=n_heads, hd=hd,
                          scale=hd ** -0.5 * LOG2E),
        grid=(n_real + n_fill_rows // qb,),
        in_specs=in_specs,
        out_specs=pl.BlockSpec((qb, d), lambda step: (step, 0)),
        out_shape=jax.ShapeDtypeStruct((n_tok_total, d), BF16),
        compiler_params=_params(("arbitrary",), 56),
        name="attn_prompt",
    )(*args)


def _attn_sample_kernel(*refs, n_seg, **kw):
    _attn_heads(refs[0], refs[1:1 + 2 * n_seg], refs[1 + 2 * n_seg], refs[-1], 0, n_seg=n_seg, **kw)


def _attn_sample(q, kv, cache_k, cache_v, table, row0, batch, seq, n_heads, o_buf):
    d = q.shape[1]
    hd = d // n_heads
    past = cache_k.shape[1]
    assert row0 % seq == 0
    rb = row0 // seq
    assert seq <= CHUNK and past == LEFT_LEN
    bias = _band_bias(table, seq, past + seq, past)
    in_specs = [
        pl.BlockSpec((seq, d), lambda b, i: (rb + b, 0)),
        pl.BlockSpec((1, past, d), lambda b, i: (b, 0, 0)),
        pl.BlockSpec((1, past, d), lambda b, i: (b, 0, 0)),
        pl.BlockSpec((seq, d), lambda b, i: (rb + b, 0)),
        pl.BlockSpec((seq, d), lambda b, i: (rb + b, 1)),
        pl.BlockSpec((n_heads, seq, past + seq), lambda b, i: (0, 0, 0)),
        pl.BlockSpec(memory_space=pl.ANY),
    ]
    return pl.pallas_call(
        functools.partial(_attn_sample_kernel, n_seg=2, n_heads=n_heads, hd=hd, scale=hd ** -0.5 * LOG2E,
                          clamp_mask=False),
        grid=(batch, 1),
        in_specs=in_specs,
        out_specs=pl.BlockSpec((seq, d), lambda b, i: (rb + b, 0)),
        out_shape=jax.ShapeDtypeStruct(o_buf.shape, BF16),
        input_output_aliases={6: 0},
        compiler_params=_params(("parallel", "arbitrary"), 56),
        name="attn_sample",
    )(q, cache_k, cache_v, kv, kv, bias, o_buf)


def kernel(x_prompt, x_sample, state_pool, cache_k, cache_v, g_pool, w_pool, s_pool, g_kv, w_kv, g_attn, w_q, w_o, rel_bias, g_ffn, w_group, b_group, w_router, b_router, w_gate, w_up, w_down, g_final):
    bp, lp, d = x_prompt.shape
    bs, ls, _ = x_sample.shape
    depth = g_ffn.shape[0]
    n_pool = g_pool.shape[0]
    n_heads, hd = cache_k.shape[2], cache_k.shape[3]
    n_groups, n_experts = w_group.shape[-1], w_router.shape[-1]
    tp, tsm = bp * lp, bs * ls
    t = tp + tsm
    tb = 128

    w_rt = jnp.concatenate([w_group, w_router], axis=-1)
    w_rt = jnp.pad(w_rt, ((0, 0), (0, 0), (0, LANES - w_rt.shape[-1]))).astype(BF16)
    b_rt = jnp.concatenate([b_group, b_router], axis=-1)
    b_rt = jnp.pad(b_rt, ((0, 0), (0, LANES - b_rt.shape[-1])))[:, None, :]
    moe_p = dict(n_groups=n_groups, n_experts=n_experts, g_ffn=g_ffn, w_rt=w_rt, b_rt=b_rt,
                 w_gate=w_gate.astype(BF16), w_up=w_up.astype(BF16), w_down=w_down.astype(BF16))
    w_pool_bf = w_pool.astype(BF16)
    w_kv_bf = w_kv.astype(BF16)
    w_q_bf = w_q.astype(BF16)
    w_o_bf = w_o.astype(BF16)

    zero_halo = jnp.zeros((1, HALO, d), F32)
    samp_halo = jnp.pad(state_pool, ((0, 0), (0, 0), (1, 0), (0, 0)))

    ck = cache_k.reshape(bs, cache_k.shape[1], d).astype(BF16)
    cv = cache_v.reshape(bs, cache_v.shape[1], d).astype(BF16)

    x = None
    hist_p, hist_s = [], []
    kv_f32 = kv_bf = None
    keep = min(LEFT_LEN, lp)
    for layer in range(depth):
        if layer < n_pool:
            if layer == 0:
                src_p, off_p, src_s, off_s = x_prompt.reshape(tp, d), 0, x_sample.reshape(tsm, d), 0
            else:
                src_p, off_p, src_s, off_s = x, 0, x, tp
            buf, hp = _pool_layer(src_p, off_p, None, 0, t, bp, lp, zero_halo, g_pool[layer], w_pool_bf[layer],
                                  s_pool[layer], 0, n_fill_rows=tsm)
            x, hs = _pool_layer(src_s, off_s, buf, tp, t, bs, ls, samp_halo[layer], g_pool[layer], w_pool_bf[layer],
                                s_pool[layer], PAST_LEN)
            hist_p.append(hp)
            hist_s.append(hs)
        else:
            j = layer - n_pool
            if j == 0:
                kv_f32, kv_bf = _rms_proj(x, t, g_kv, w_kv_bf, (F32, BF16), keep=(bp, lp, keep))
            (q,) = _rms_proj(x, t, g_attn[j], w_q_bf[j], (BF16,))
            o = _attn_prompt(q, kv_bf, rel_bias[j], bp, lp, n_heads, t)
            o = _attn_sample(q, kv_bf, ck, cv, rel_bias[j], tp, bs, ls, n_heads, o)
            x = _proj_res(o, w_o_bf[j], x, t)
        x = _hier_moe(x, t, layer, moe_p, tb, g_final if layer == depth - 1 else None)

    y_prompt = x[:tp].reshape(bp, lp, d)
    y_sample = x[tp:t].reshape(bs, ls, d)
    if kv_f32.shape[0] == bp * keep + tsm:
        kvp = kv_f32[:bp * keep].reshape(bp, keep, 2 * d)
        kvs = kv_f32[bp * keep:].reshape(bs, ls, 2 * d)
    else:
        kvp = kv_f32[:tp].reshape(bp, lp, 2 * d)[:, lp - keep:]
        kvs = kv_f32[tp:].reshape(bs, ls, 2 * d)
    return (y_prompt, y_sample, jnp.stack(hist_p, axis=0), jnp.stack(hist_s, axis=0),
            kvp[..., :d].reshape(bp, keep, n_heads, hd), kvp[..., d:].reshape(bp, keep, n_heads, hd),
            kvs[..., :d].reshape(bs, ls, n_heads, hd), kvs[..., d:].reshape(bs, ls, n_heads, hd))
```

```python
import functools

import jax
import jax.numpy as jnp
from jax import lax
from jax.experimental import pallas as pl
from jax.experimental.pallas import tpu as pltpu

RMS_EPS = 1e-6
NEG_INF = -1e30
CHUNK = 64
LEFT_CHUNKS = 8
LEFT_LEN = LEFT_CHUNKS * CHUNK
POOL_WINDOWS = (2, 4, 8, 16)
POOL_HIST = max(POOL_WINDOWS) - 1
HALO = POOL_HIST + 1
PAST_LEN = 1024
LANES = 128
LOG2E = 1.4426950408889634

SEQ_TILE_ROWS = (256, 128, 64)
TOKEN_TILE_ROWS = (512, 256, 128)
DISPATCH_TILE_ROWS = (256, 128)
PANEL_COLS = (2048, 1024, 512, 256, 128)
MOE_BLOCK_ROWS = 128
VMEM_MIB = 48
VMEM_WIDE_MIB = 56

F32 = jnp.float32
BF16 = jnp.bfloat16


def _pick(n, cands):
    for c in cands:
        if n % c == 0:
            return c
    raise ValueError(f"no tile for {n} in {cands}")


def _params(sem, vmem_mib=None):
    return pltpu.CompilerParams(
        dimension_semantics=sem,
        vmem_limit_bytes=None if vmem_mib is None else vmem_mib * 1024 * 1024)


def _rms(x, g):
    ms = jnp.mean(x * x, axis=-1, keepdims=True)
    return (x * lax.rsqrt(ms + RMS_EPS)) * g


def _pool_kernel(*refs, ts, ns, n_real, dg, pos0, aliased):
    if aliased:
        x_ref, halo0_ref, g_ref, w_ref, s_ref, _, out_ref, hist_ref, halo_sc = refs
    else:
        x_ref, halo0_ref, g_ref, w_ref, s_ref, out_ref, hist_ref, halo_sc = refs
    step = pl.program_id(0)
    s = lax.rem(step, ns)

    @pl.when(step >= n_real)
    def _():
        out_ref[...] = jnp.zeros_like(out_ref)

    @pl.when(step < n_real)
    def _():
        @pl.when(s == 0)
        def _():
            halo_sc[...] = halo0_ref[0]

        x = x_ref[...]
        h = _rms(x, g_ref[...])
        halo = halo_sc[...]
        pos = lax.broadcasted_iota(jnp.int32, (ts, 1), 0) + (s * ts + pos0)
        for gi, w in enumerate(POOL_WINDOWS):
            lo, hi = gi * dg, (gi + 1) * dg
            hg = h[:, lo:hi]
            acc = jnp.concatenate([halo[:, lo:hi], hg], axis=0)
            k = 1
            while k < w:
                acc = acc + pltpu.roll(acc, k, 0)
                k *= 2
            cnt = jnp.minimum(pos + 1, w).astype(F32)
            pooled = acc[HALO:, :] * (1.0 / cnt) - hg
            mix = jnp.dot(pooled.astype(BF16), w_ref[gi], preferred_element_type=F32) * s_ref[:, lo:hi]
            out_ref[:, lo:hi] = x[:, lo:hi] + mix
        halo_sc[...] = h[ts - HALO:, :]

        @pl.when(s == ns - 1)
        def _():
            hist_ref[0] = halo_sc[1:HALO, :]


def _pool_layer(x_in, in_off, out_buf, out_off, n_tok_total, batch, seq, halo0, g, w_bf, s, pos0, n_fill_rows=0):
    d = x_in.shape[1]
    dg = d // len(POOL_WINDOWS)
    ts = _pick(seq, SEQ_TILE_ROWS)
    ns = seq // ts
    assert in_off % ts == 0 and out_off % ts == 0 and n_fill_rows % ts == 0
    ib, ob = in_off // ts, out_off // ts
    n_real = batch * ns
    aliased = out_buf is not None
    halo_b = halo0.shape[0]

    def row(i):
        return jnp.minimum(i, n_real - 1) // ns

    in_specs = [
        pl.BlockSpec((ts, d), lambda i: (ib + jnp.minimum(i, n_real - 1), 0)),
        pl.BlockSpec((1, HALO, d), (lambda i: (row(i), 0, 0)) if halo_b > 1 else (lambda i: (0, 0, 0))),
        pl.BlockSpec((1, d), lambda i: (0, 0)),
        pl.BlockSpec((len(POOL_WINDOWS), dg, dg), lambda i: (0, 0, 0)),
        pl.BlockSpec((1, d), lambda i: (0, 0)),
    ]
    args = [x_in, halo0, g.reshape(1, d), w_bf, s.reshape(1, d)]
    if aliased:
        in_specs.append(pl.BlockSpec(memory_space=pl.ANY))
        args.append(out_buf)
    return pl.pallas_call(
        functools.partial(_pool_kernel, ts=ts, ns=ns, n_real=n_real, dg=dg, pos0=pos0, aliased=aliased),
        grid=(n_real + n_fill_rows // ts,),
        in_specs=in_specs,
        out_specs=[
            pl.BlockSpec((ts, d), lambda i: (ob + i, 0)),
            pl.BlockSpec((1, POOL_HIST, d), lambda i: (row(i), 0, 0)),
        ],
        out_shape=[
            jax.ShapeDtypeStruct((n_tok_total, d), F32),
            jax.ShapeDtypeStruct((batch, POOL_HIST, d), F32),
        ],
        scratch_shapes=[pltpu.VMEM((HALO, d), F32)],
        input_output_aliases={len(args) - 1: 0} if aliased else {},
        compiler_params=_params(("arbitrary",), VMEM_MIB),
        name="pool_layer",
    )(*args)


def _router_kernel(x_ref, g_ref, w_ref, b_ref, route_ref, cnt_ref, carry_sc, *, tm, n_groups, per_group):
    i = pl.program_id(0)

    @pl.when(i == 0)
    def _():
        carry_sc[...] = jnp.zeros_like(carry_sc)

    h = _rms(x_ref[...], g_ref[...])
    lg = jnp.dot(h.astype(BF16), w_ref[...], preferred_element_type=F32) + b_ref[...]
    lane = lax.broadcasted_iota(jnp.int32, (tm, LANES), 1).astype(F32)
    big = float(LANES)

    def first_max(vals, mask):
        v = jnp.max(jnp.where(mask, vals, -jnp.inf), axis=-1, keepdims=True)
        idx = jnp.min(jnp.where(mask & (vals == v), lane, big), axis=-1, keepdims=True)
        return v, idx

    gmask = lane < n_groups
    _, gsel = first_max(lg, gmask)
    lo = n_groups + gsel * per_group
    emask = (lane >= lo) & (lane < lo + per_group)
    _, i1 = first_max(lg, emask)
    _, i2 = first_max(lg, emask & (lane != i1))

    a = jnp.minimum(i1, i2) - lo
    b = jnp.maximum(i1, i2) - lo
    n_pairs = per_group * (per_group - 1) // 2
    key = gsel * n_pairs + a * (2 * per_group - 1 - a) * 0.5 + (b - a - 1.0)
    sel = lane == key
    onehot = jnp.where(sel, 1.0, 0.0)
    r_i = lax.broadcasted_iota(jnp.int32, (tm, tm), 0)
    c_i = lax.broadcasted_iota(jnp.int32, (tm, tm), 1)
    lower = jnp.where(c_i < r_i, 1.0, 0.0).astype(BF16)
    before = jnp.dot(lower, onehot.astype(BF16), preferred_element_type=F32) + carry_sc[...]
    rank = jnp.sum(jnp.where(sel, before, 0.0), axis=-1, keepdims=True)
    carry = carry_sc[...] + jnp.sum(onehot, axis=0, keepdims=True)
    carry_sc[...] = carry
    cnt_ref[...] = carry

    route_ref[...] = jnp.where(lane == 0, key, jnp.where(lane == 1, rank, 0.0))


def _route(x, t, g, w_rt, b_rt, n_groups, per_group):
    d = x.shape[1]
    tm = _pick(t, TOKEN_TILE_ROWS)
    return pl.pallas_call(
        functools.partial(_router_kernel, tm=tm, n_groups=n_groups, per_group=per_group),
        grid=(t // tm,),
        in_specs=[
            pl.BlockSpec((tm, d), lambda i: (i, 0)),
            pl.BlockSpec((1, d), lambda i: (0, 0)),
            pl.BlockSpec((d, LANES), lambda i: (0, 0)),
            pl.BlockSpec((1, LANES), lambda i: (0, 0)),
        ],
        out_specs=[
            pl.BlockSpec((tm, LANES), lambda i: (i, 0)),
            pl.BlockSpec((1, LANES), lambda i: (0, 0)),
        ],
        out_shape=[
            jax.ShapeDtypeStruct((t, LANES), F32),
            jax.ShapeDtypeStruct((1, LANES), F32),
        ],
        scratch_shapes=[pltpu.VMEM((1, LANES), F32)],
        compiler_params=_params(("arbitrary",), VMEM_MIB),
        name="moe_route",
    )(x, g.reshape(1, d), w_rt, b_rt)


def _dispatch_kernel(fill_ref, dest_ref, x_ref, xs_hbm, zbuf, stage, zsem, ssem, *, tm, tb):
    i = pl.program_id(0)
    n = pl.num_programs(0)
    nb = fill_ref.shape[0]
    slot = lax.rem(i, 2)

    def fill_copy(blk):
        return pltpu.make_async_copy(zbuf, xs_hbm.at[pl.ds(pl.multiple_of(blk * tb, tb), tb), :], zsem.at[0])

    def rows_wait(s):
        pltpu.make_async_copy(stage.at[s], xs_hbm.at[pl.ds(0, tm), :], ssem.at[s]).wait()

    @pl.when(i == 0)
    def _():
        zbuf[...] = jnp.zeros_like(zbuf)

        def start(blk, c):
            @pl.when(fill_ref[blk] > 0)
            def _():
                fill_copy(blk).start()
            return c

        def wait(blk, c):
            @pl.when(fill_ref[blk] > 0)
            def _():
                fill_copy(blk).wait()
            return c

        lax.fori_loop(0, nb, start, 0)
        lax.fori_loop(0, nb, wait, 0)

    @pl.when(i >= 2)
    def _():
        rows_wait(slot)

    stage[slot] = x_ref[...]
    for r in range(tm):
        pltpu.make_async_copy(stage.at[slot, pl.ds(r, 1), :], xs_hbm.at[pl.ds(dest_ref[0, 0, r], 1), :],
                              ssem.at[slot]).start()

    @pl.when(i == n - 1)
    def _():
        @pl.when(i >= 1)
        def _():
            rows_wait(1 - slot)
        rows_wait(slot)


def _moe_dispatch(x, t, dest, blk_fill, tb):
    d = x.shape[1]
    tm = _pick(t, DISPATCH_TILE_ROWS)
    nt = t // tm
    nb = blk_fill.shape[0]
    grid_spec = pltpu.PrefetchScalarGridSpec(
        num_scalar_prefetch=1,
        grid=(nt,),
        in_specs=[
            pl.BlockSpec((1, 1, tm), lambda i, fill: (i, 0, 0), memory_space=pltpu.SMEM),
            pl.BlockSpec((tm, d), lambda i, fill: (i, 0)),
        ],
        out_specs=pl.BlockSpec(memory_space=pl.ANY),
        scratch_shapes=[pltpu.VMEM((tb, d), F32), pltpu.VMEM((2, tm, d), F32),
                        pltpu.SemaphoreType.DMA((1,)), pltpu.SemaphoreType.DMA((2,))],
    )
    return pl.pallas_call(
        functools.partial(_dispatch_kernel, tm=tm, tb=tb),
        grid_spec=grid_spec,
        out_shape=jax.ShapeDtypeStruct((nb * tb, d), F32),
        compiler_params=_params(("arbitrary",), VMEM_MIB),
        name="moe_dispatch",
    )(blk_fill, dest.reshape(nt, 1, tm), x)


def _moe_kernel(ea_ref, eb_ref, nv_ref, src_ref, dst_ref, xs_ref, g_ref, wrt_ref, brt_ref,
                wga_ref, wua_ref, wda_ref, wgb_ref, wub_ref, wdb_ref, *rest, tb, n_tok, n_groups, final):
    if final:
        gf_ref, out_hbm, obuf, ssem = rest
    else:
        out_hbm, obuf, ssem = rest
    b = pl.program_id(0)
    nb = pl.num_programs(0)
    slot = lax.rem(b, 2)
    nv = nv_ref[b]
    nv_prev2 = nv_ref[jnp.maximum(b - 2, 0)]
    nv_prev1 = nv_ref[jnp.maximum(b - 1, 0)]

    def scatter_wait(s):
        pltpu.make_async_copy(obuf.at[s], out_hbm.at[pl.ds(0, tb), :], ssem.at[s]).wait()

    @pl.when(b == 0)
    def _():
        obuf[...] = jnp.zeros_like(obuf)
        for s in range(2):
            spare = pltpu.make_async_copy(obuf.at[s], out_hbm.at[pl.ds(n_tok + s * tb, tb), :], ssem.at[s])
            spare.start()
            spare.wait()

    def scatter(idx_ref, s):
        for r in range(tb):
            pltpu.make_async_copy(obuf.at[s, pl.ds(r, 1), :], out_hbm.at[pl.ds(idx_ref[0, 0, r], 1), :],
                                  ssem.at[s]).start()

    @pl.when((b >= 2) & (nv_prev2 > 0))
    def _():
        scatter_wait(slot)

    @pl.when(nv > 0)
    def _():
        x = xs_ref[...]
        hn = _rms(x, g_ref[...]).astype(BF16)

        def expert(wg_ref, wu_ref, wd_ref):
            a = jnp.dot(hn, wg_ref[...], preferred_element_type=F32)
            u = jnp.dot(hn, wu_ref[...], preferred_element_type=F32)
            act = (a * (1.0 / (1.0 + jnp.exp(-a)))) * u
            return jnp.dot(act.astype(BF16), wd_ref[...], preferred_element_type=F32)

        lg = jnp.dot(hn, wrt_ref[...], preferred_element_type=F32) + brt_ref[...]
        lane = lax.broadcasted_iota(jnp.int32, (tb, LANES), 1)
        gmask = lane < n_groups
        gmax = jnp.max(jnp.where(gmask, lg, -jnp.inf), axis=-1, keepdims=True)
        g_prob = 1.0 / jnp.sum(jnp.where(gmask, jnp.exp(lg - gmax), 0.0), axis=-1, keepdims=True)
        la = jnp.sum(jnp.where(lane == n_groups + ea_ref[b], lg, 0.0), axis=-1, keepdims=True)
        lb = jnp.sum(jnp.where(lane == n_groups + eb_ref[b], lg, 0.0), axis=-1, keepdims=True)
        top = jnp.maximum(la, lb)
        pa = jnp.exp(la - top)
        pb = jnp.exp(lb - top)
        den = pa + pb
        gate_a = (pa / den) * g_prob
        gate_b = (pb / den) * g_prob

        ya = expert(wga_ref, wua_ref, wda_ref)
        yb = expert(wgb_ref, wub_ref, wdb_ref)
        out = x + (gate_a * ya + gate_b * yb)
        if final:
            out = _rms(out, gf_ref[...])
        obuf[slot] = out
        scatter(dst_ref, slot)

    @pl.when(b == nb - 1)
    def _():
        @pl.when((b >= 1) & (nv_prev1 > 0))
        def _():
            scatter_wait(1 - slot)

        @pl.when(nv > 0)
        def _():
            scatter_wait(slot)


def _moe_mlp(xs, t, g, w_rt, b_rt, n_groups, wg, wu, wd, layer, blk_ea, blk_eb, blk_nv, blk_src, slot_dst, tb,
             g_final):
    d = wg.shape[-2]
    de = wg.shape[-1]
    nb = blk_nv.shape[0]
    dst3 = slot_dst.reshape(nb, 1, tb)
    final = g_final is not None

    def wspec(which, shape):
        return pl.BlockSpec((None, None) + shape, lambda b, ea, eb, nv, src: (layer, (ea, eb)[which][b], 0, 0))

    in_specs = [
        pl.BlockSpec((1, 1, tb), lambda b, ea, eb, nv, src: (b, 0, 0), memory_space=pltpu.SMEM),
        pl.BlockSpec((tb, d), lambda b, ea, eb, nv, src: (src[b], 0)),
        pl.BlockSpec((1, d), lambda b, ea, eb, nv, src: (0, 0)),
        pl.BlockSpec((d, LANES), lambda b, ea, eb, nv, src: (0, 0)),
        pl.BlockSpec((1, LANES), lambda b, ea, eb, nv, src: (0, 0)),
        wspec(0, (d, de)), wspec(0, (d, de)), wspec(0, (de, d)),
        wspec(1, (d, de)), wspec(1, (d, de)), wspec(1, (de, d)),
    ]
    args = [blk_ea, blk_eb, blk_nv, blk_src, dst3, xs, g.reshape(1, d), w_rt, b_rt, wg, wu, wd, wg, wu, wd]
    if final:
        in_specs.append(pl.BlockSpec((1, d), lambda b, ea, eb, nv, src: (0, 0)))
        args.append(g_final.reshape(1, d))
    grid_spec = pltpu.PrefetchScalarGridSpec(
        num_scalar_prefetch=4,
        grid=(nb,),
        in_specs=in_specs,
        out_specs=pl.BlockSpec(memory_space=pl.ANY),
        scratch_shapes=[pltpu.VMEM((2, tb, d), F32), pltpu.SemaphoreType.DMA((2,))],
    )
    return pl.pallas_call(
        functools.partial(_moe_kernel, tb=tb, n_tok=t, n_groups=n_groups, final=final),
        grid_spec=grid_spec,
        out_shape=jax.ShapeDtypeStruct((t + 2 * tb, d), F32),
        compiler_params=_params(("arbitrary",), VMEM_WIDE_MIB),
        name="moe_mlp",
    )(*args)


def _pair_tables(n_groups, per_group):
    ea, eb = [], []
    for grp in range(n_groups):
        for a in range(per_group):
            for b in range(a + 1, per_group):
                ea.append(grp * per_group + a)
                eb.append(grp * per_group + b)
    return jnp.asarray(ea, jnp.int32), jnp.asarray(eb, jnp.int32)


def _hier_moe(x, t, layer, p, tb, g_final=None):
    n_groups, n_experts = p["n_groups"], p["n_experts"]
    per_group = n_experts // n_groups
    n_buckets = n_groups * (per_group * (per_group - 1) // 2)
    assert n_buckets <= LANES
    route, cnt = _route(x, t, p["g_ffn"][layer], p["w_rt"][layer], p["b_rt"][layer], n_groups, per_group)

    key = route[:, 0].astype(jnp.int32)
    rank = route[:, 1].astype(jnp.int32)
    counts = cnt[0, :n_buckets].astype(jnp.int32)
    padded = (counts + tb - 1) // tb * tb
    pend = jnp.cumsum(padded)
    pstart = pend - padded
    bucket_ids = jnp.arange(n_buckets, dtype=jnp.int32)
    dest = jnp.sum(jnp.where(key[:, None] == bucket_ids[None, :], pstart[None, :], 0), axis=1) + rank
    n_blocks = -(-(t + n_buckets * (tb - 1)) // tb)
    n_slots = n_blocks * tb
    blk_start = jnp.arange(n_blocks, dtype=jnp.int32) * tb
    blk_bucket = jnp.minimum(jnp.sum((pend[None, :] <= blk_start[:, None]).astype(jnp.int32), axis=1), n_buckets - 1)
    blk_nv = jnp.clip(counts[blk_bucket] - (blk_start - pstart[blk_bucket]), 0, tb).astype(jnp.int32)
    tab_a, tab_b = _pair_tables(n_groups, per_group)
    blk_ea, blk_eb = tab_a[blk_bucket], tab_b[blk_bucket]

    slot_tok = jnp.zeros((n_slots,), jnp.int32).at[dest].set(jnp.arange(t, dtype=jnp.int32))
    slot = jnp.arange(n_slots, dtype=jnp.int32)
    in_blk, blk = slot % tb, slot // tb
    valid = in_blk < jnp.repeat(blk_nv, tb)
    slot_dst = jnp.where(valid, slot_tok, t + (blk % 2) * tb + in_blk)
    blk_fill = (blk_nv < tb).astype(jnp.int32)
    n_used = jnp.sum((blk_nv > 0).astype(jnp.int32))
    blk_src = jnp.minimum(jnp.arange(n_blocks, dtype=jnp.int32), jnp.maximum(n_used - 1, 0))

    xs = _moe_dispatch(x, t, dest, blk_fill, tb)
    return _moe_mlp(xs, t, p["g_ffn"][layer], p["w_rt"][layer], p["b_rt"][layer], n_groups, p["w_gate"], p["w_up"],
                    p["w_down"], layer, blk_ea, blk_eb, blk_nv, blk_src, slot_dst, tb, g_final)


def _rms_proj_kernel(x_ref, g_ref, w_ref, *out_refs):
    res = jnp.dot(_rms(x_ref[...], g_ref[...]).astype(BF16), w_ref[...], preferred_element_type=F32)
    for o in out_refs:
        o[...] = res.astype(o.dtype)


def _proj_tiles(t, n):
    return _pick(t, TOKEN_TILE_ROWS), _pick(n, PANEL_COLS)


def _rms_proj(x, t, g, w_bf, out_dtypes):
    d = x.shape[1]
    n = w_bf.shape[1]
    tm, tn = _proj_tiles(t, n)
    return pl.pallas_call(
        _rms_proj_kernel,
        grid=(n // tn, t // tm),
        in_specs=[
            pl.BlockSpec((tm, d), lambda j, i: (i, 0)),
            pl.BlockSpec((1, d), lambda j, i: (0, 0)),
            pl.BlockSpec((d, tn), lambda j, i: (0, j)),
        ],
        out_specs=[pl.BlockSpec((tm, tn), lambda j, i: (i, j)) for _ in out_dtypes],
        out_shape=[jax.ShapeDtypeStruct((t, n), dt) for dt in out_dtypes],
        compiler_params=_params(("parallel", "parallel"), VMEM_WIDE_MIB),
        name="rms_proj",
    )(x, g.reshape(1, d), w_bf)


def _proj_res_kernel(o_ref, w_ref, x_ref, out_ref):
    out_ref[...] = x_ref[...] + jnp.dot(o_ref[...], w_ref[...], preferred_element_type=F32)


def _proj_res(o_bf, w_bf, x, t):
    d = x.shape[1]
    k = o_bf.shape[1]
    tm, tn = _proj_tiles(t, d)
    return pl.pallas_call(
        _proj_res_kernel,
        grid=(d // tn, t // tm),
        in_specs=[
            pl.BlockSpec((tm, k), lambda j, i: (i, 0)),
            pl.BlockSpec((k, tn), lambda j, i: (0, j)),
            pl.BlockSpec((tm, tn), lambda j, i: (i, j)),
        ],
        out_specs=pl.BlockSpec((tm, tn), lambda j, i: (i, j)),
        out_shape=jax.ShapeDtypeStruct((t, d), F32),
        compiler_params=_params(("parallel", "parallel"), VMEM_WIDE_MIB),
        name="proj_res",
    )(o_bf, w_bf, x)


def _attn_heads(q_ref, kv_refs, bias_ref, o_ref, i, *, n_seg, n_heads, hd, scale, clamp_mask):
    def seg(ref, h):
        sl = slice(h * hd, (h + 1) * hd)
        return ref[0, :, sl] if len(ref.shape) == 3 else ref[:, sl]

    for h in range(n_heads):
        sl = slice(h * hd, (h + 1) * hd)
        qh = q_ref[:, sl]
        scores = []
        off = 0
        for si in range(n_seg):
            kh = seg(kv_refs[2 * si], h)
            nk = kh.shape[0]
            s = lax.dot_general(qh, kh, (((1,), (1,)), ((), ())), preferred_element_type=F32) * scale
            s = s + bias_ref[h, :, off:off + nk]
            if clamp_mask and si < n_seg - 1:
                s = jnp.where(i >= n_seg - 1 - si, s, NEG_INF)
            scores.append(s)
            off += nk
        m = functools.reduce(jnp.maximum, [jnp.max(s, axis=-1, keepdims=True) for s in scores])
        ps = [jnp.exp2(s - m) for s in scores]
        l = functools.reduce(lambda a, b: a + b, [jnp.sum(p, axis=-1, keepdims=True) for p in ps])
        acc = None
        for si in range(n_seg):
            pv = jnp.dot(ps[si].astype(BF16), seg(kv_refs[2 * si + 1], h), preferred_element_type=F32)
            acc = pv if acc is None else acc + pv
        o_ref[:, sl] = (acc / l).astype(o_ref.dtype)


def _attn_prompt_kernel(*refs, nq, n_real, n_seg, **kw):
    q_ref, kv_refs, bias_ref, o_ref = refs[0], refs[1:1 + 2 * n_seg], refs[1 + 2 * n_seg], refs[-1]
    step = pl.program_id(0)

    @pl.when(step >= n_real)
    def _():
        o_ref[...] = jnp.zeros_like(o_ref)

    @pl.when(step < n_real)
    def _():
        _attn_heads(q_ref, kv_refs, bias_ref, o_ref, lax.rem(step, nq), n_seg=n_seg, **kw)

def _band_bias(table, qb, n_keys, left):
    clip = (table.shape[-1] - 1) // 2
    n_heads = table.shape[0]
    w = qb + n_keys
    m = jnp.arange(w, dtype=jnp.int32)
    delta = jnp.where(m < n_keys, m, m - w)
    u = table[:, jnp.clip(left - delta, -clip, clip) + clip].astype(F32)
    toe = jnp.tile(u, (1, qb))[:, :qb * (w - 1)].reshape(n_heads, qb, w - 1)[:, :, :n_keys]
    i = jnp.arange(qb, dtype=jnp.int32)[:, None]
    j = jnp.arange(n_keys, dtype=jnp.int32)[None, :]
    kpos = j - left
    cq = (i // CHUNK) * CHUNK
    inband = (kpos >= cq - LEFT_LEN) & (kpos < cq + CHUNK)
    return jnp.where(inband[None], toe * LOG2E, NEG_INF)


def _attn_prompt(q, kv, table, batch, seq, n_heads, n_tok_total):
    d = q.shape[1]
    hd = d // n_heads
    qb = _pick(seq, SEQ_TILE_ROWS)
    nq = seq // qb
    n_seg = LEFT_LEN // qb + 1
    n_real = batch * nq
    n_fill_rows = n_tok_total - batch * seq
    assert n_fill_rows % qb == 0
    bias = _band_bias(table, qb, n_seg * qb, LEFT_LEN)

    def kv_spec(si, col):
        back = n_seg - 1 - si

        def index(step):
            r = jnp.minimum(step, n_real - 1)
            return (r - jnp.minimum(r % nq, back), col)
        return pl.BlockSpec((qb, d), index)

    in_specs = [pl.BlockSpec((qb, d), lambda step: (jnp.minimum(step, n_real - 1), 0))]
    args = [q]
    for si in range(n_seg):
        in_specs += [kv_spec(si, 0), kv_spec(si, 1)]
        args += [kv, kv]
    in_specs.append(pl.BlockSpec((n_heads, qb, n_seg * qb), lambda step: (0, 0, 0)))
    args.append(bias)
    return pl.pallas_call(
        functools.partial(_attn_prompt_kernel, nq=nq, n_real=n_real, n_seg=n_seg, n_heads=n_heads, hd=hd,
                          scale=hd ** -0.5 * LOG2E, clamp_mask=True),
        grid=(n_real + n_fill_rows // qb,),
        in_specs=in_specs,
        out_specs=pl.BlockSpec((qb, d), lambda step: (step, 0)),
        out_shape=jax.ShapeDtypeStruct((n_tok_total, d), BF16),
        compiler_params=_params(("arbitrary",), VMEM_WIDE_MIB),
        name="attn_prompt",
    )(*args)


def _attn_sample_kernel(*refs, n_seg, **kw):
    _attn_heads(refs[0], refs[1:1 + 2 * n_seg], refs[1 + 2 * n_seg], refs[-1], 0, n_seg=n_seg, **kw)


def _attn_sample(q, kv, cache_k, cache_v, table, row0, batch, seq, n_heads, o_buf):
    d = q.shape[1]
    hd = d // n_heads
    past = cache_k.shape[1]
    assert row0 % seq == 0
    rb = row0 // seq
    assert seq <= CHUNK and past == LEFT_LEN
    bias = _band_bias(table, seq, past + seq, past)
    in_specs = [
        pl.BlockSpec((seq, d), lambda b, i: (rb + b, 0)),
        pl.BlockSpec((1, past, d), lambda b, i: (b, 0, 0)),
        pl.BlockSpec((1, past, d), lambda b, i: (b, 0, 0)),
        pl.BlockSpec((seq, d), lambda b, i: (rb + b, 0)),
        pl.BlockSpec((seq, d), lambda b, i: (rb + b, 1)),
        pl.BlockSpec((n_heads, seq, past + seq), lambda b, i: (0, 0, 0)),
        pl.BlockSpec(memory_space=pl.ANY),
    ]
    return pl.pallas_call(
        functools.partial(_attn_sample_kernel, n_seg=2, n_heads=n_heads, hd=hd, scale=hd ** -0.5 * LOG2E,
                          clamp_mask=False),
        grid=(batch, 1),
        in_specs=in_specs,
        out_specs=pl.BlockSpec((seq, d), lambda b, i: (rb + b, 0)),
        out_shape=jax.ShapeDtypeStruct(o_buf.shape, BF16),
        input_output_aliases={6: 0},
        compiler_params=_params(("parallel", "arbitrary"), VMEM_WIDE_MIB),
        name="attn_sample",
    )(q, cache_k, cache_v, kv, kv, bias, o_buf)


def kernel(x_prompt, x_sample, state_pool, cache_k, cache_v, g_pool, w_pool, s_pool, g_kv, w_kv, g_attn, w_q, w_o, rel_bias, g_ffn, w_group, b_group, w_router, b_router, w_gate, w_up, w_down, g_final):
    bp, lp, d = x_prompt.shape
    bs, ls, _ = x_sample.shape
    depth = g_ffn.shape[0]
    n_pool = g_pool.shape[0]
    n_heads, hd = cache_k.shape[2], cache_k.shape[3]
    n_groups, n_experts = w_group.shape[-1], w_router.shape[-1]
    tp, tsm = bp * lp, bs * ls
    t = tp + tsm
    tb = MOE_BLOCK_ROWS

    w_rt = jnp.concatenate([w_group, w_router], axis=-1)
    w_rt = jnp.pad(w_rt, ((0, 0), (0, 0), (0, LANES - w_rt.shape[-1]))).astype(BF16)
    b_rt = jnp.concatenate([b_group, b_router], axis=-1)
    b_rt = jnp.pad(b_rt, ((0, 0), (0, LANES - b_rt.shape[-1])))[:, None, :]
    moe_p = dict(n_groups=n_groups, n_experts=n_experts, g_ffn=g_ffn, w_rt=w_rt, b_rt=b_rt,
                 w_gate=w_gate.astype(BF16), w_up=w_up.astype(BF16), w_down=w_down.astype(BF16))
    w_pool_bf = w_pool.astype(BF16)
    w_kv_bf = w_kv.astype(BF16)
    w_q_bf = w_q.astype(BF16)
    w_o_bf = w_o.astype(BF16)

    zero_halo = jnp.zeros((1, HALO, d), F32)
    samp_halo = jnp.pad(state_pool, ((0, 0), (0, 0), (1, 0), (0, 0)))

    ck = cache_k.reshape(bs, cache_k.shape[1], d).astype(BF16)
    cv = cache_v.reshape(bs, cache_v.shape[1], d).astype(BF16)

    x = None
    hist_p, hist_s = [], []
    kv_f32 = kv_bf = None
    for layer in range(depth):
        if layer < n_pool:
            if layer == 0:
                src_p, off_p, src_s, off_s = x_prompt.reshape(tp, d), 0, x_sample.reshape(tsm, d), 0
            else:
                src_p, off_p, src_s, off_s = x, 0, x, tp
            buf, hp = _pool_layer(src_p, off_p, None, 0, t, bp, lp, zero_halo, g_pool[layer], w_pool_bf[layer],
                                  s_pool[layer], 0, n_fill_rows=tsm)
            x, hs = _pool_layer(src_s, off_s, buf, tp, t, bs, ls, samp_halo[layer], g_pool[layer], w_pool_bf[layer],
                                s_pool[layer], PAST_LEN)
            hist_p.append(hp)
            hist_s.append(hs)
        else:
            j = layer - n_pool
            if j == 0:
                kv_f32, kv_bf = _rms_proj(x, t, g_kv, w_kv_bf, (F32, BF16))
            (q,) = _rms_proj(x, t, g_attn[j], w_q_bf[j], (BF16,))
            o = _attn_prompt(q, kv_bf, rel_bias[j], bp, lp, n_heads, t)
            o = _attn_sample(q, kv_bf, ck, cv, rel_bias[j], tp, bs, ls, n_heads, o)
            x = _proj_res(o, w_o_bf[j], x, t)
        x = _hier_moe(x, t, layer, moe_p, tb, g_final if layer == depth - 1 else None)

    y_prompt = x[:tp].reshape(bp, lp, d)
    y_sample = x[tp:t].reshape(bs, ls, d)
    keep = min(LEFT_LEN, lp)
    if tsm % lp == 0:
        kvp = kv_f32.reshape(t // lp, lp, 2 * d)[:bp, lp - keep:]
    else:
        kvp = kv_f32[:tp].reshape(bp, lp, 2 * d)[:, lp - keep:]
    kvs = kv_f32[tp:].reshape(bs, ls, 2 * d)
    return (y_prompt, y_sample, jnp.stack(hist_p, axis=0), jnp.stack(hist_s, axis=0),
            kvp[..., :d].reshape(bp, keep, n_heads, hd), kvp[..., d:].reshape(bp, keep, n_heads, hd),
            kvs[..., :d].reshape(bs, ls, n_heads, hd), kvs[..., d:].reshape(bs, ls, n_heads, hd))
```

```python
import functools

import jax
import jax.numpy as jnp
from jax import lax
from jax.experimental import pallas as pl
from jax.experimental.pallas import tpu as pltpu

RMS_EPS = 1e-6
NEG_INF = -1e30
CHUNK = 64
LEFT_CHUNKS = 8
LEFT_LEN = LEFT_CHUNKS * CHUNK
POOL_WINDOWS = (2, 4, 8, 16)
POOL_HIST = max(POOL_WINDOWS) - 1
HALO = POOL_HIST + 1
PAST_LEN = 1024
LANES = 128
LOG2E = 1.4426950408889634

SEQ_TILE_ROWS = (256, 128, 64)
TOKEN_TILE_ROWS = (512, 256, 128)
DISPATCH_TILE_ROWS = (256, 128)
PANEL_COLS = (2048, 1024, 512, 256, 128)
MOE_BLOCK_ROWS = 128
VMEM_MIB = 48
VMEM_WIDE_MIB = 56

F32 = jnp.float32
BF16 = jnp.bfloat16


def _pick(n, cands):
    for c in cands:
        if n % c == 0:
            return c
    raise ValueError(f"no tile for {n} in {cands}")


def _params(sem, vmem_mib=None):
    return pltpu.CompilerParams(
        dimension_semantics=sem,
        vmem_limit_bytes=None if vmem_mib is None else vmem_mib * 1024 * 1024)


def _rms(x, g):
    ms = jnp.mean(x * x, axis=-1, keepdims=True)
    return (x * lax.rsqrt(ms + RMS_EPS)) * g


def _pool_kernel(*refs, ts, ns, n_real, dg, pos0, aliased):
    if aliased:
        x_ref, halo0_ref, g_ref, w_ref, s_ref, _, out_ref, hist_ref, halo_sc = refs
    else:
        x_ref, halo0_ref, g_ref, w_ref, s_ref, out_ref, hist_ref, halo_sc = refs
    step = pl.program_id(0)
    s = lax.rem(step, ns)

    @pl.when(step >= n_real)
    def _():
        out_ref[...] = jnp.zeros_like(out_ref)

    @pl.when(step < n_real)
    def _():
        @pl.when(s == 0)
        def _():
            halo_sc[...] = halo0_ref[0]

        x = x_ref[...]
        h = _rms(x, g_ref[...])
        halo = halo_sc[...]
        pos = lax.broadcasted_iota(jnp.int32, (ts, 1), 0) + (s * ts + pos0)
        for gi, w in enumerate(POOL_WINDOWS):
            lo, hi = gi * dg, (gi + 1) * dg
            hg = h[:, lo:hi]
            acc = jnp.concatenate([halo[:, lo:hi], hg], axis=0)
            k = 1
            while k < w:
                acc = acc + pltpu.roll(acc, k, 0)
                k *= 2
            cnt = jnp.minimum(pos + 1, w).astype(F32)
            pooled = acc[HALO:, :] * (1.0 / cnt) - hg
            mix = jnp.dot(pooled.astype(BF16), w_ref[gi], preferred_element_type=F32) * s_ref[:, lo:hi]
            out_ref[:, lo:hi] = x[:, lo:hi] + mix
        halo_sc[...] = h[ts - HALO:, :]

        @pl.when(s == ns - 1)
        def _():
            hist_ref[0] = halo_sc[1:HALO, :]


def _pool_layer(x_in, in_off, out_buf, out_off, n_tok_total, batch, seq, halo0, g, w_bf, s, pos0, n_fill_rows=0):
    d = x_in.shape[1]
    dg = d // len(POOL_WINDOWS)
    ts = _pick(seq, SEQ_TILE_ROWS)
    ns = seq // ts
    assert in_off % ts == 0 and out_off % ts == 0 and n_fill_rows % ts == 0
    ib, ob = in_off // ts, out_off // ts
    n_real = batch * ns
    aliased = out_buf is not None
    halo_b = halo0.shape[0]

    def row(i):
        return jnp.minimum(i, n_real - 1) // ns

    in_specs = [
        pl.BlockSpec((ts, d), lambda i: (ib + jnp.minimum(i, n_real - 1), 0)),
        pl.BlockSpec((1, HALO, d), (lambda i: (row(i), 0, 0)) if halo_b > 1 else (lambda i: (0, 0, 0))),
        pl.BlockSpec((1, d), lambda i: (0, 0)),
        pl.BlockSpec((len(POOL_WINDOWS), dg, dg), lambda i: (0, 0, 0)),
        pl.BlockSpec((1, d), lambda i: (0, 0)),
    ]
    args = [x_in, halo0, g.reshape(1, d), w_bf, s.reshape(1, d)]
    if aliased:
        in_specs.append(pl.BlockSpec(memory_space=pl.ANY))
        args.append(out_buf)
    return pl.pallas_call(
        functools.partial(_pool_kernel, ts=ts, ns=ns, n_real=n_real, dg=dg, pos0=pos0, aliased=aliased),
        grid=(n_real + n_fill_rows // ts,),
        in_specs=in_specs,
        out_specs=[
            pl.BlockSpec((ts, d), lambda i: (ob + i, 0)),
            pl.BlockSpec((1, POOL_HIST, d), lambda i: (row(i), 0, 0)),
        ],
        out_shape=[
            jax.ShapeDtypeStruct((n_tok_total, d), F32),
            jax.ShapeDtypeStruct((batch, POOL_HIST, d), F32),
        ],
        scratch_shapes=[pltpu.VMEM((HALO, d), F32)],
        input_output_aliases={len(args) - 1: 0} if aliased else {},
        compiler_params=_params(("arbitrary",), VMEM_MIB),
        name="pool_layer",
    )(*args)


def _router_kernel(x_ref, g_ref, w_ref, b_ref, route_ref, cnt_ref, carry_sc, *, tm, n_groups, per_group):
    i = pl.program_id(0)

    @pl.when(i == 0)
    def _():
        carry_sc[...] = jnp.zeros_like(carry_sc)

    h = _rms(x_ref[...], g_ref[...])
    lg = jnp.dot(h.astype(BF16), w_ref[...], preferred_element_type=F32) + b_ref[...]
    lane = lax.broadcasted_iota(jnp.int32, (tm, LANES), 1).astype(F32)
    big = float(LANES)

    def first_max(vals, mask):
        v = jnp.max(jnp.where(mask, vals, -jnp.inf), axis=-1, keepdims=True)
        idx = jnp.min(jnp.where(mask & (vals == v), lane, big), axis=-1, keepdims=True)
        return v, idx

    gmask = lane < n_groups
    _, gsel = first_max(lg, gmask)
    lo = n_groups + gsel * per_group
    emask = (lane >= lo) & (lane < lo + per_group)
    _, i1 = first_max(lg, emask)
    _, i2 = first_max(lg, emask & (lane != i1))

    a = jnp.minimum(i1, i2) - lo
    b = jnp.maximum(i1, i2) - lo
    n_pairs = per_group * (per_group - 1) // 2
    key = gsel * n_pairs + a * (2 * per_group - 1 - a) * 0.5 + (b - a - 1.0)
    sel = lane == key
    onehot = jnp.where(sel, 1.0, 0.0)
    r_i = lax.broadcasted_iota(jnp.int32, (tm, tm), 0)
    c_i = lax.broadcasted_iota(jnp.int32, (tm, tm), 1)
    lower = jnp.where(c_i < r_i, 1.0, 0.0).astype(BF16)
    before = jnp.dot(lower, onehot.astype(BF16), preferred_element_type=F32) + carry_sc[...]
    rank = jnp.sum(jnp.where(sel, before, 0.0), axis=-1, keepdims=True)
    carry = carry_sc[...] + jnp.sum(onehot, axis=0, keepdims=True)
    carry_sc[...] = carry
    cnt_ref[...] = carry

    route_ref[...] = jnp.where(lane == 0, key, jnp.where(lane == 1, rank, 0.0))


def _route(x, t, g, w_rt, b_rt, n_groups, per_group):
    d = x.shape[1]
    tm = _pick(t, TOKEN_TILE_ROWS)
    return pl.pallas_call(
        functools.partial(_router_kernel, tm=tm, n_groups=n_groups, per_group=per_group),
        grid=(t // tm,),
        in_specs=[
            pl.BlockSpec((tm, d), lambda i: (i, 0)),
            pl.BlockSpec((1, d), lambda i: (0, 0)),
            pl.BlockSpec((d, LANES), lambda i: (0, 0)),
            pl.BlockSpec((1, LANES), lambda i: (0, 0)),
        ],
        out_specs=[
            pl.BlockSpec((tm, LANES), lambda i: (i, 0)),
            pl.BlockSpec((1, LANES), lambda i: (0, 0)),
        ],
        out_shape=[
            jax.ShapeDtypeStruct((t, LANES), F32),
            jax.ShapeDtypeStruct((1, LANES), F32),
        ],
        scratch_shapes=[pltpu.VMEM((1, LANES), F32)],
        compiler_params=_params(("arbitrary",), VMEM_MIB),
        name="moe_route",
    )(x, g.reshape(1, d), w_rt, b_rt)


def _dispatch_kernel(fill_ref, dest_ref, x_ref, xs_hbm, zbuf, stage, zsem, ssem, *, tm, tb):
    i = pl.program_id(0)
    n = pl.num_programs(0)
    nb = fill_ref.shape[0]
    slot = lax.rem(i, 2)

    def fill_copy(blk):
        return pltpu.make_async_copy(zbuf, xs_hbm.at[pl.ds(pl.multiple_of(blk * tb, tb), tb), :], zsem.at[0])

    def rows_wait(s):
        pltpu.make_async_copy(stage.at[s], xs_hbm.at[pl.ds(0, tm), :], ssem.at[s]).wait()

    @pl.when(i == 0)
    def _():
        zbuf[...] = jnp.zeros_like(zbuf)

        def start(blk, c):
            @pl.when(fill_ref[blk] > 0)
            def _():
                fill_copy(blk).start()
            return c

        def wait(blk, c):
            @pl.when(fill_ref[blk] > 0)
            def _():
                fill_copy(blk).wait()
            return c

        lax.fori_loop(0, nb, start, 0)
        lax.fori_loop(0, nb, wait, 0)

    @pl.when(i >= 2)
    def _():
        rows_wait(slot)

    stage[slot] = x_ref[...]
    for r in range(tm):
        pltpu.make_async_copy(stage.at[slot, pl.ds(r, 1), :], xs_hbm.at[pl.ds(dest_ref[0, 0, r], 1), :],
                              ssem.at[slot]).start()

    @pl.when(i == n - 1)
    def _():
        @pl.when(i >= 1)
        def _():
            rows_wait(1 - slot)
        rows_wait(slot)


def _moe_dispatch(x, t, dest, blk_fill, tb):
    d = x.shape[1]
    tm = _pick(t, DISPATCH_TILE_ROWS)
    nt = t // tm
    nb = blk_fill.shape[0]
    grid_spec = pltpu.PrefetchScalarGridSpec(
        num_scalar_prefetch=1,
        grid=(nt,),
        in_specs=[
            pl.BlockSpec((1, 1, tm), lambda i, fill: (i, 0, 0), memory_space=pltpu.SMEM),
            pl.BlockSpec((tm, d), lambda i, fill: (i, 0)),
        ],
        out_specs=pl.BlockSpec(memory_space=pl.ANY),
        scratch_shapes=[pltpu.VMEM((tb, d), F32), pltpu.VMEM((2, tm, d), F32),
                        pltpu.SemaphoreType.DMA((1,)), pltpu.SemaphoreType.DMA((2,))],
    )
    return pl.pallas_call(
        functools.partial(_dispatch_kernel, tm=tm, tb=tb),
        grid_spec=grid_spec,
        out_shape=jax.ShapeDtypeStruct((nb * tb, d), F32),
        compiler_params=_params(("arbitrary",), VMEM_MIB),
        name="moe_dispatch",
    )(blk_fill, dest.reshape(nt, 1, tm), x)


def _moe_kernel(ea_ref, eb_ref, nv_ref, src_ref, dst_ref, xs_ref, g_ref, wrt_ref, brt_ref,
                wga_ref, wua_ref, wda_ref, wgb_ref, wub_ref, wdb_ref, *rest, tb, n_tok, n_groups, final):
    if final:
        gf_ref, out_hbm, obuf, ssem = rest
    else:
        out_hbm, obuf, ssem = rest
    b = pl.program_id(0)
    nb = pl.num_programs(0)
    slot = lax.rem(b, 2)
    nv = nv_ref[b]
    nv_prev2 = nv_ref[jnp.maximum(b - 2, 0)]
    nv_prev1 = nv_ref[jnp.maximum(b - 1, 0)]

    def scatter_wait(s):
        pltpu.make_async_copy(obuf.at[s], out_hbm.at[pl.ds(0, tb), :], ssem.at[s]).wait()

    @pl.when(b == 0)
    def _():
        obuf[...] = jnp.zeros_like(obuf)
        for s in range(2):
            spare = pltpu.make_async_copy(obuf.at[s], out_hbm.at[pl.ds(n_tok + s * tb, tb), :], ssem.at[s])
            spare.start()
            spare.wait()

    def scatter(idx_ref, s):
        for r in range(tb):
            pltpu.make_async_copy(obuf.at[s, pl.ds(r, 1), :], out_hbm.at[pl.ds(idx_ref[0, 0, r], 1), :],
                                  ssem.at[s]).start(priority=1)

    @pl.when((b >= 2) & (nv_prev2 > 0))
    def _():
        scatter_wait(slot)

    @pl.when(nv > 0)
    def _():
        x = xs_ref[...]
        hn = _rms(x, g_ref[...]).astype(BF16)

        def expert(wg_ref, wu_ref, wd_ref):
            a = jnp.dot(hn, wg_ref[...], preferred_element_type=F32)
            u = jnp.dot(hn, wu_ref[...], preferred_element_type=F32)
            act = (a * (1.0 / (1.0 + jnp.exp(-a)))) * u
            return jnp.dot(act.astype(BF16), wd_ref[...], preferred_element_type=F32)

        lg = jnp.dot(hn, wrt_ref[...], preferred_element_type=F32) + brt_ref[...]
        lane = lax.broadcasted_iota(jnp.int32, (tb, LANES), 1)
        gmask = lane < n_groups
        gmax = jnp.max(jnp.where(gmask, lg, -jnp.inf), axis=-1, keepdims=True)
        g_prob = 1.0 / jnp.sum(jnp.where(gmask, jnp.exp(lg - gmax), 0.0), axis=-1, keepdims=True)
        la = jnp.sum(jnp.where(lane == n_groups + ea_ref[b], lg, 0.0), axis=-1, keepdims=True)
        lb = jnp.sum(jnp.where(lane == n_groups + eb_ref[b], lg, 0.0), axis=-1, keepdims=True)
        top = jnp.maximum(la, lb)
        pa = jnp.exp(la - top)
        pb = jnp.exp(lb - top)
        den = pa + pb
        gate_a = (pa / den) * g_prob
        gate_b = (pb / den) * g_prob

        ya = expert(wga_ref, wua_ref, wda_ref)
        yb = expert(wgb_ref, wub_ref, wdb_ref)
        out = x + (gate_a * ya + gate_b * yb)
        if final:
            out = _rms(out, gf_ref[...])
        obuf[slot] = out
        scatter(dst_ref, slot)

    @pl.when(b == nb - 1)
    def _():
        @pl.when((b >= 1) & (nv_prev1 > 0))
        def _():
            scatter_wait(1 - slot)

        @pl.when(nv > 0)
        def _():
            scatter_wait(slot)


def _moe_mlp(xs, t, g, w_rt, b_rt, n_groups, wg, wu, wd, layer, blk_ea, blk_eb, blk_nv, blk_src, slot_dst, tb,
             g_final):
    d = wg.shape[-2]
    de = wg.shape[-1]
    nb = blk_nv.shape[0]
    dst3 = slot_dst.reshape(nb, 1, tb)
    final = g_final is not None

    def wspec(which, shape):
        return pl.BlockSpec((None, None) + shape, lambda b, ea, eb, nv, src: (layer, (ea, eb)[which][b], 0, 0))

    in_specs = [
        pl.BlockSpec((1, 1, tb), lambda b, ea, eb, nv, src: (b, 0, 0), memory_space=pltpu.SMEM),
        pl.BlockSpec((tb, d), lambda b, ea, eb, nv, src: (src[b], 0)),
        pl.BlockSpec((1, d), lambda b, ea, eb, nv, src: (0, 0)),
        pl.BlockSpec((d, LANES), lambda b, ea, eb, nv, src: (0, 0)),
        pl.BlockSpec((1, LANES), lambda b, ea, eb, nv, src: (0, 0)),
        wspec(0, (d, de)), wspec(0, (d, de)), wspec(0, (de, d)),
        wspec(1, (d, de)), wspec(1, (d, de)), wspec(1, (de, d)),
    ]
    args = [blk_ea, blk_eb, blk_nv, blk_src, dst3, xs, g.reshape(1, d), w_rt, b_rt, wg, wu, wd, wg, wu, wd]
    if final:
        in_specs.append(pl.BlockSpec((1, d), lambda b, ea, eb, nv, src: (0, 0)))
        args.append(g_final.reshape(1, d))
    grid_spec = pltpu.PrefetchScalarGridSpec(
        num_scalar_prefetch=4,
        grid=(nb,),
        in_specs=in_specs,
        out_specs=pl.BlockSpec(memory_space=pl.ANY),
        scratch_shapes=[pltpu.VMEM((2, tb, d), F32), pltpu.SemaphoreType.DMA((2,))],
    )
    return pl.pallas_call(
        functools.partial(_moe_kernel, tb=tb, n_tok=t, n_groups=n_groups, final=final),
        grid_spec=grid_spec,
        out_shape=jax.ShapeDtypeStruct((t + 2 * tb, d), F32),
        compiler_params=_params(("arbitrary",), VMEM_WIDE_MIB),
        name="moe_mlp",
    )(*args)


def _pair_tables(n_groups, per_group):
    ea, eb = [], []
    for grp in range(n_groups):
        for a in range(per_group):
            for b in range(a + 1, per_group):
                ea.append(grp * per_group + a)
                eb.append(grp * per_group + b)
    return jnp.asarray(ea, jnp.int32), jnp.asarray(eb, jnp.int32)


def _hier_moe(x, t, layer, p, tb, g_final=None):
    n_groups, n_experts = p["n_groups"], p["n_experts"]
    per_group = n_experts // n_groups
    n_buckets = n_groups * (per_group * (per_group - 1) // 2)
    assert n_buckets <= LANES
    route, cnt = _route(x, t, p["g_ffn"][layer], p["w_rt"][layer], p["b_rt"][layer], n_groups, per_group)

    key = route[:, 0].astype(jnp.int32)
    rank = route[:, 1].astype(jnp.int32)
    counts = cnt[0, :n_buckets].astype(jnp.int32)
    padded = (counts + tb - 1) // tb * tb
    pend = jnp.cumsum(padded)
    pstart = pend - padded
    bucket_ids = jnp.arange(n_buckets, dtype=jnp.int32)
    dest = jnp.sum(jnp.where(key[:, None] == bucket_ids[None, :], pstart[None, :], 0), axis=1) + rank
    n_blocks = -(-(t + n_buckets * (tb - 1)) // tb)
    n_slots = n_blocks * tb
    blk_start = jnp.arange(n_blocks, dtype=jnp.int32) * tb
    blk_bucket = jnp.minimum(jnp.sum((pend[None, :] <= blk_start[:, None]).astype(jnp.int32), axis=1), n_buckets - 1)
    blk_nv = jnp.clip(counts[blk_bucket] - (blk_start - pstart[blk_bucket]), 0, tb).astype(jnp.int32)
    tab_a, tab_b = _pair_tables(n_groups, per_group)
    blk_ea, blk_eb = tab_a[blk_bucket], tab_b[blk_bucket]

    slot_tok = jnp.zeros((n_slots,), jnp.int32).at[dest].set(jnp.arange(t, dtype=jnp.int32))
    slot = jnp.arange(n_slots, dtype=jnp.int32)
    in_blk, blk = slot % tb, slot // tb
    valid = in_blk < jnp.repeat(blk_nv, tb)
    slot_dst = jnp.where(valid, slot_tok, t + (blk % 2) * tb + in_blk)
    blk_fill = (blk_nv < tb).astype(jnp.int32)
    n_used = jnp.sum((blk_nv > 0).astype(jnp.int32))
    blk_src = jnp.minimum(jnp.arange(n_blocks, dtype=jnp.int32), jnp.maximum(n_used - 1, 0))

    xs = _moe_dispatch(x, t, dest, blk_fill, tb)
    return _moe_mlp(xs, t, p["g_ffn"][layer], p["w_rt"][layer], p["b_rt"][layer], n_groups, p["w_gate"], p["w_up"],
                    p["w_down"], layer, blk_ea, blk_eb, blk_nv, blk_src, slot_dst, tb, g_final)


def _rms_proj_kernel(x_ref, g_ref, w_ref, *out_refs):
    res = jnp.dot(_rms(x_ref[...], g_ref[...]).astype(BF16), w_ref[...], preferred_element_type=F32)
    for o in out_refs:
        o[...] = res.astype(o.dtype)


def _proj_tiles(t, n):
    return _pick(t, TOKEN_TILE_ROWS), _pick(n, PANEL_COLS)


def _rms_proj(x, t, g, w_bf, out_dtypes):
    d = x.shape[1]
    n = w_bf.shape[1]
    tm, tn = _proj_tiles(t, n)
    return pl.pallas_call(
        _rms_proj_kernel,
        grid=(n // tn, t // tm),
        in_specs=[
            pl.BlockSpec((tm, d), lambda j, i: (i, 0)),
            pl.BlockSpec((1, d), lambda j, i: (0, 0)),
            pl.BlockSpec((d, tn), lambda j, i: (0, j)),
        ],
        out_specs=[pl.BlockSpec((tm, tn), lambda j, i: (i, j)) for _ in out_dtypes],
        out_shape=[jax.ShapeDtypeStruct((t, n), dt) for dt in out_dtypes],
        compiler_params=_params(("parallel", "parallel"), VMEM_WIDE_MIB),
        name="rms_proj",
    )(x, g.reshape(1, d), w_bf)


def _proj_res_kernel(o_ref, w_ref, x_ref, out_ref):
    out_ref[...] = x_ref[...] + jnp.dot(o_ref[...], w_ref[...], preferred_element_type=F32)


def _proj_res(o_bf, w_bf, x, t):
    d = x.shape[1]
    k = o_bf.shape[1]
    tm, tn = _proj_tiles(t, d)
    return pl.pallas_call(
        _proj_res_kernel,
        grid=(d // tn, t // tm),
        in_specs=[
            pl.BlockSpec((tm, k), lambda j, i: (i, 0)),
            pl.BlockSpec((k, tn), lambda j, i: (0, j)),
            pl.BlockSpec((tm, tn), lambda j, i: (i, j)),
        ],
        out_specs=pl.BlockSpec((tm, tn), lambda j, i: (i, j)),
        out_shape=jax.ShapeDtypeStruct((t, d), F32),
        compiler_params=_params(("parallel", "parallel"), VMEM_WIDE_MIB),
        name="proj_res",
    )(o_bf, w_bf, x)


def _attn_heads(q_ref, kv_refs, bias_ref, o_ref, i, *, n_seg, n_heads, hd, scale, clamp_mask):
    def seg(ref, h):
        sl = slice(h * hd, (h + 1) * hd)
        return ref[0, :, sl] if len(ref.shape) == 3 else ref[:, sl]

    for h in range(n_heads):
        sl = slice(h * hd, (h + 1) * hd)
        qh = q_ref[:, sl]
        scores = []
        off = 0
        for si in range(n_seg):
            kh = seg(kv_refs[2 * si], h)
            nk = kh.shape[0]
            s = lax.dot_general(qh, kh, (((1,), (1,)), ((), ())), preferred_element_type=F32) * scale
            s = s + bias_ref[h, :, off:off + nk]
            if clamp_mask and si < n_seg - 1:
                s = jnp.where(i >= n_seg - 1 - si, s, NEG_INF)
            scores.append(s)
            off += nk
        m = functools.reduce(jnp.maximum, [jnp.max(s, axis=-1, keepdims=True) for s in scores])
        ps = [jnp.exp2(s - m) for s in scores]
        l = functools.reduce(lambda a, b: a + b, [jnp.sum(p, axis=-1, keepdims=True) for p in ps])
        acc = None
        for si in range(n_seg):
            pv = jnp.dot(ps[si].astype(BF16), seg(kv_refs[2 * si + 1], h), preferred_element_type=F32)
            acc = pv if acc is None else acc + pv
        o_ref[:, sl] = (acc / l).astype(o_ref.dtype)


def _attn_prompt_kernel(*refs, nq, n_real, n_seg, **kw):
    q_ref, kv_refs, bias_ref, o_ref = refs[0], refs[1:1 + 2 * n_seg], refs[1 + 2 * n_seg], refs[-1]
    step = pl.program_id(0)

    @pl.when(step >= n_real)
    def _():
        o_ref[...] = jnp.zeros_like(o_ref)

    @pl.when(step < n_real)
    def _():
        _attn_heads(q_ref, kv_refs, bias_ref, o_ref, lax.rem(step, nq), n_seg=n_seg, **kw)

def _band_bias(table, qb, n_keys, left):
    clip = (table.shape[-1] - 1) // 2
    n_heads = table.shape[0]
    w = qb + n_keys
    m = jnp.arange(w, dtype=jnp.int32)
    delta = jnp.where(m < n_keys, m, m - w)
    u = table[:, jnp.clip(left - delta, -clip, clip) + clip].astype(F32)
    toe = jnp.tile(u, (1, qb))[:, :qb * (w - 1)].reshape(n_heads, qb, w - 1)[:, :, :n_keys]
    i = jnp.arange(qb, dtype=jnp.int32)[:, None]
    j = jnp.arange(n_keys, dtype=jnp.int32)[None, :]
    kpos = j - left
    cq = (i // CHUNK) * CHUNK
    inband = (kpos >= cq - LEFT_LEN) & (kpos < cq + CHUNK)
    return jnp.where(inband[None], toe * LOG2E, NEG_INF)


def _attn_prompt(q, kv, table, batch, seq, n_heads, n_tok_total):
    d = q.shape[1]
    hd = d // n_heads
    qb = _pick(seq, SEQ_TILE_ROWS)
    nq = seq // qb
    n_seg = LEFT_LEN // qb + 1
    n_real = batch * nq
    n_fill_rows = n_tok_total - batch * seq
    assert n_fill_rows % qb == 0
    bias = _band_bias(table, qb, n_seg * qb, LEFT_LEN)

    def kv_spec(si, col):
        back = n_seg - 1 - si

        def index(step):
            r = jnp.minimum(step, n_real - 1)
            return (r - jnp.minimum(r % nq, back), col)
        return pl.BlockSpec((qb, d), index)

    in_specs = [pl.BlockSpec((qb, d), lambda step: (jnp.minimum(step, n_real - 1), 0))]
    args = [q]
    for si in range(n_seg):
        in_specs += [kv_spec(si, 0), kv_spec(si, 1)]
        args += [kv, kv]
    in_specs.append(pl.BlockSpec((n_heads, qb, n_seg * qb), lambda step: (0, 0, 0)))
    args.append(bias)
    return pl.pallas_call(
        functools.partial(_attn_prompt_kernel, nq=nq, n_real=n_real, n_seg=n_seg, n_heads=n_heads, hd=hd,
                          scale=hd ** -0.5 * LOG2E, clamp_mask=True),
        grid=(n_real + n_fill_rows // qb,),
        in_specs=in_specs,
        out_specs=pl.BlockSpec((qb, d), lambda step: (step, 0)),
        out_shape=jax.ShapeDtypeStruct((n_tok_total, d), BF16),
        compiler_params=_params(("arbitrary",), VMEM_WIDE_MIB),
        name="attn_prompt",
    )(*args)


def _attn_sample_kernel(*refs, n_seg, **kw):
    _attn_heads(refs[0], refs[1:1 + 2 * n_seg], refs[1 + 2 * n_seg], refs[-1], 0, n_seg=n_seg, **kw)


def _attn_sample(q, kv, cache_k, cache_v, table, row0, batch, seq, n_heads, o_buf):
    d = q.shape[1]
    hd = d // n_heads
    past = cache_k.shape[1]
    assert row0 % seq == 0
    rb = row0 // seq
    assert seq <= CHUNK and past == LEFT_LEN
    bias = _band_bias(table, seq, past + seq, past)
    in_specs = [
        pl.BlockSpec((seq, d), lambda b, i: (rb + b, 0)),
        pl.BlockSpec((1, past, d), lambda b, i: (b, 0, 0)),
        pl.BlockSpec((1, past, d), lambda b, i: (b, 0, 0)),
        pl.BlockSpec((seq, d), lambda b, i: (rb + b, 0)),
        pl.BlockSpec((seq, d), lambda b, i: (rb + b, 1)),
        pl.BlockSpec((n_heads, seq, past + seq), lambda b, i: (0, 0, 0)),
        pl.BlockSpec(memory_space=pl.ANY),
    ]
    return pl.pallas_call(
        functools.partial(_attn_sample_kernel, n_seg=2, n_heads=n_heads, hd=hd, scale=hd ** -0.5 * LOG2E,
                          clamp_mask=False),
        grid=(batch, 1),
        in_specs=in_specs,
        out_specs=pl.BlockSpec((seq, d), lambda b, i: (rb + b, 0)),
        out_shape=jax.ShapeDtypeStruct(o_buf.shape, BF16),
        input_output_aliases={6: 0},
        compiler_params=_params(("parallel", "arbitrary"), VMEM_WIDE_MIB),
        name="attn_sample",
    )(q, cache_k, cache_v, kv, kv, bias, o_buf)


def kernel(x_prompt, x_sample, state_pool, cache_k, cache_v, g_pool, w_pool, s_pool, g_kv, w_kv, g_attn, w_q, w_o, rel_bias, g_ffn, w_group, b_group, w_router, b_router, w_gate, w_up, w_down, g_final):
    bp, lp, d = x_prompt.shape
    bs, ls, _ = x_sample.shape
    depth = g_ffn.shape[0]
    n_pool = g_pool.shape[0]
    n_heads, hd = cache_k.shape[2], cache_k.shape[3]
    n_groups, n_experts = w_group.shape[-1], w_router.shape[-1]
    tp, tsm = bp * lp, bs * ls
    t = tp + tsm
    tb = MOE_BLOCK_ROWS

    w_rt = jnp.concatenate([w_group, w_router], axis=-1)
    w_rt = jnp.pad(w_rt, ((0, 0), (0, 0), (0, LANES - w_rt.shape[-1]))).astype(BF16)
    b_rt = jnp.concatenate([b_group, b_router], axis=-1)
    b_rt = jnp.pad(b_rt, ((0, 0), (0, LANES - b_rt.shape[-1])))[:, None, :]
    moe_p = dict(n_groups=n_groups, n_experts=n_experts, g_ffn=g_ffn, w_rt=w_rt, b_rt=b_rt,
                 w_gate=w_gate.astype(BF16), w_up=w_up.astype(BF16), w_down=w_down.astype(BF16))
    w_pool_bf = w_pool.astype(BF16)
    w_kv_bf = w_kv.astype(BF16)
    w_q_bf = w_q.astype(BF16)
    w_o_bf = w_o.astype(BF16)

    zero_halo = jnp.zeros((1, HALO, d), F32)
    samp_halo = jnp.pad(state_pool, ((0, 0), (0, 0), (1, 0), (0, 0)))

    ck = cache_k.reshape(bs, cache_k.shape[1], d).astype(BF16)
    cv = cache_v.reshape(bs, cache_v.shape[1], d).astype(BF16)

    x = None
    hist_p, hist_s = [], []
    kv_f32 = kv_bf = None
    for layer in range(depth):
        if layer < n_pool:
            if layer == 0:
                src_p, off_p, src_s, off_s = x_prompt.reshape(tp, d), 0, x_sample.reshape(tsm, d), 0
            else:
                src_p, off_p, src_s, off_s = x, 0, x, tp
            buf, hp = _pool_layer(src_p, off_p, None, 0, t, bp, lp, zero_halo, g_pool[layer], w_pool_bf[layer],
                                  s_pool[layer], 0, n_fill_rows=tsm)
            x, hs = _pool_layer(src_s, off_s, buf, tp, t, bs, ls, samp_halo[layer], g_pool[layer], w_pool_bf[layer],
                                s_pool[layer], PAST_LEN)
            hist_p.append(hp)
            hist_s.append(hs)
        else:
            j = layer - n_pool
            if j == 0:
                kv_f32, kv_bf = _rms_proj(x, t, g_kv, w_kv_bf, (F32, BF16))
            (q,) = _rms_proj(x, t, g_attn[j], w_q_bf[j], (BF16,))
            o = _attn_prompt(q, kv_bf, rel_bias[j], bp, lp, n_heads, t)
            o = _attn_sample(q, kv_bf, ck, cv, rel_bias[j], tp, bs, ls, n_heads, o)
            x = _proj_res(o, w_o_bf[j], x, t)
        x = _hier_moe(x, t, layer, moe_p, tb, g_final if layer == depth - 1 else None)

    y_prompt = x[:tp].reshape(bp, lp, d)
    y_sample = x[tp:t].reshape(bs, ls, d)
    keep = min(LEFT_LEN, lp)
    if tsm % lp == 0:
        kvp = kv_f32.reshape(t // lp, lp, 2 * d)[:bp, lp - keep:]
    else:
        kvp = kv_f32[:tp].reshape(bp, lp, 2 * d)[:, lp - keep:]
    kvs = kv_f32[tp:].reshape(bs, ls, 2 * d)
    return (y_prompt, y_sample, jnp.stack(hist_p, axis=0), jnp.stack(hist_s, axis=0),
            kvp[..., :d].reshape(bp, keep, n_heads, hd), kvp[..., d:].reshape(bp, keep, n_heads, hd),
            kvs[..., :d].reshape(bs, ls, n_heads, hd), kvs[..., d:].reshape(bs, ls, n_heads, hd))
```
